```python
import math
import jax
import jax.numpy as jnp
from jax import lax
import numpy as np

D_MODEL = 2048
BATCH = 1
SEQ = 16384
DEPTH = 4
DEC_BATCH = 4
DEC_SEQ = 2048
PAST_LEN = 128

N_BRANCH = 4
W_BR = D_MODEL // 2
LRU_BLOCKS = 8
LRU_BS = W_BR // LRU_BLOCKS
CONV_W = 4
CONV_PAD = (2, 1)
LRU_C = 8.0
RWKV_HEAD = 64
RWKV_H = W_BR // RWKV_HEAD
RWKV_LORA_W = 64
RWKV_LORA_A = 64
RWKV_LN_EPS = RWKV_HEAD * 1e-5
RET_H = 4
RET_DK = W_BR // 2 // RET_H
RET_DV = W_BR // RET_H
RET_CHUNK = 128
RET_GN_EPS = 1e-5
ROPE_BASE = 10000.0
S5_P = 16
S5_G = W_BR // S5_P
S5_N = 64
LAM_RE_MAX = -1e-4
NORM_EPS = 1e-6

SPLITS = (
    W_BR, W_BR,
    W_BR, W_BR, W_BR, RWKV_LORA_W, RWKV_LORA_W, RWKV_LORA_A, W_BR,
    RET_H * RET_DK, RET_H * RET_DK, W_BR, W_BR,
    W_BR, W_BR,
    N_BRANCH * D_MODEL,
)
N_IN = sum(SPLITS)

kernel_name = 'hybrid_bidir_lru_rwkv7_retnet_s5_encoder'


def _rmsnorm(x, g):
    xf = x.astype(jnp.float32)
    y = xf * lax.rsqrt(jnp.mean(xf * xf, axis=-1, keepdims=True) + NORM_EPS)
    return (y * g.astype(jnp.float32)).astype(x.dtype)


def _head_norm(y, eps):
    mean = jnp.mean(y, axis=-1, keepdims=True)
    yc = y - mean
    var = jnp.mean(yc * yc, axis=-1, keepdims=True)
    return yc * lax.rsqrt(var + eps)


def _split_cols(p):
    outs = []
    start = 0
    for n in SPLITS:
        outs.append(p[..., start:start + n])
        start += n
    return outs


def _linear_scan_combine(e1, e2):
    a1, b1 = e1
    a2, b2 = e2
    return a1 * a2, a2 * b1 + b2


def _centred_shift(x):
    prev = jnp.pad(x[:, :-1], ((0, 0), (1, 0), (0, 0)))
    nxt = jnp.pad(x[:, 1:], ((0, 0), (0, 1), (0, 0)))
    return 0.5 * (prev + nxt)


def _rotary(x, pos):
    half = x.shape[-1] // 2
    inv = ROPE_BASE ** (-jnp.arange(half, dtype=jnp.float32) / half)
    ang = pos[:, None] * inv[None, :]
    cos = jnp.cos(ang)[None, :, None, :]
    sin = jnp.sin(ang)[None, :, None, :]
    x1, x2 = x[..., :half], x[..., half:]
    return jnp.concatenate([x1 * cos - x2 * sin, x1 * sin + x2 * cos], axis=-1)


def _rglru_branch(xa, conv_w, conv_b, w_r, b_r, w_i, b_i, lam):
    f32 = jnp.float32
    bsz, seqlen, _ = xa.shape
    xc = lax.conv_general_dilated(
        xa.astype(f32), conv_w.astype(f32)[:, None, :], window_strides=(1,),
        padding=[CONV_PAD], dimension_numbers=('NWC', 'WIO', 'NWC'),
        feature_group_count=W_BR) + conv_b.astype(f32)
    xblk = xc.reshape(bsz, seqlen, LRU_BLOCKS, LRU_BS)
    t = jnp.arange(seqlen)
    h_sum = jnp.zeros_like(xc)
    for d in range(2):
        r = jax.nn.sigmoid(jnp.einsum('blhi,hij->blhj', xblk, w_r[d].astype(f32)).reshape(bsz, seqlen, W_BR) + b_r[d].astype(f32))
        i = jax.nn.sigmoid(jnp.einsum('blhi,hij->blhj', xblk, w_i[d].astype(f32)).reshape(bsz, seqlen, W_BR) + b_i[d].astype(f32))
        log_a = -LRU_C * r * jax.nn.softplus(-lam[d].astype(f32))
        a = jnp.exp(log_a)
        mult = jnp.sqrt(-jnp.expm1(2.0 * log_a))
        first = 0 if d == 0 else seqlen - 1
        mult = jnp.where((t == first)[None, :, None], 1.0, mult)
        _, h = lax.associative_scan(_linear_scan_combine, (a, mult * i * xc), reverse=(d == 1), axis=1)
        h_sum = h_sum + h
    return h_sum


def _rwkv7_scan(r, w, k, v, a_vec, b_vec, reverse):
    bsz = r.shape[0]

    def step(state, inp):
        r_t, w_t, k_t, v_t, a_t, b_t = inp
        sa = jnp.einsum('bhvk,bhk->bhv', state, a_t)
        state = state * w_t[:, :, None, :] + sa[..., None] * b_t[:, :, None, :] + v_t[..., None] * k_t[:, :, None, :]
        return state, jnp.einsum('bhvk,bhk->bhv', state, r_t)

    xs = tuple(jnp.moveaxis(t, 1, 0) for t in (r, w, k, v, a_vec, b_vec))
    s0 = jnp.zeros((bsz, RWKV_H, RWKV_HEAD, RWKV_HEAD), jnp.float32)
    _, y = lax.scan(step, s0, xs, reverse=reverse)
    return jnp.moveaxis(y, 0, 1)


def _rwkv7_branch(r, k, v, wd_f, wd_b, ad, mu, w0, w2, a0, a2, k_k, k_a, r_k, lnx_g, lnx_b):
    f32 = jnp.float32
    bsz, seqlen, _ = r.shape
    mu = mu.astype(f32)
    r, k, v = [t + mu[j] * (_centred_shift(t) - t) for j, t in enumerate((r.astype(f32), k.astype(f32), v.astype(f32)))]
    a = jax.nn.sigmoid(a0.astype(f32) + ad.astype(f32) @ a2.astype(f32))
    heads = lambda t: t.reshape(bsz, seqlen, RWKV_H, RWKV_HEAD)
    kk = heads(k * k_k.astype(f32))
    kk = kk / jnp.maximum(jnp.sqrt(jnp.sum(kk * kk, axis=-1, keepdims=True)), 1e-12)
    k = k * (1.0 + (a - 1.0) * k_a.astype(f32))
    rh, kh, vh, ah = heads(r), heads(k), heads(v), heads(a)
    a_vec = -kk
    b_vec = kk * ah
    y = jnp.zeros_like(vh)
    for d, wd in enumerate((wd_f, wd_b)):
        w_log = -jax.nn.softplus(-(w0[d].astype(f32) + jnp.tanh(wd.astype(f32)) @ w2[d].astype(f32))) - 0.5
        decay = heads(jnp.exp(-jnp.exp(w_log)))
        y = y + _rwkv7_scan(rh, decay, kh, vh, a_vec, b_vec, reverse=(d == 1))
    y = _head_norm(y, RWKV_LN_EPS) * lnx_g.astype(f32).reshape(RWKV_H, RWKV_HEAD) + lnx_b.astype(f32).reshape(RWKV_H, RWKV_HEAD)
    bonus = jnp.sum(rh * kh * r_k.astype(f32), axis=-1, keepdims=True) * vh
    return (y + bonus).reshape(bsz, seqlen, W_BR)


def _retention_branch(q, k, v, gn_g):
    f32 = jnp.float32
    bsz, seqlen, _ = q.shape
    n_chunk = seqlen // RET_CHUNK
    pos_all = jnp.arange(seqlen, dtype=f32)
    q = _rotary(q.astype(f32).reshape(bsz, seqlen, RET_H, RET_DK), pos_all)
    k = _rotary(k.astype(f32).reshape(bsz, seqlen, RET_H, RET_DK), pos_all) * (RET_DK ** -0.5)
    v = v.astype(f32).reshape(bsz, seqlen, RET_H, RET_DV)
    log_g = jnp.log(1.0 - 2.0 ** (-5.0 - jnp.arange(RET_H, dtype=f32)))
    qc = q.reshape(bsz, n_chunk, RET_CHUNK, RET_H, RET_DK)
    kc = k.reshape(bsz, n_chunk, RET_CHUNK, RET_H, RET_DK)
    vc = v.reshape(bsz, n_chunk, RET_CHUNK, RET_H, RET_DV)
    pos = jnp.arange(RET_CHUNK, dtype=f32)
    intra = jnp.exp(log_g[:, None, None] * jnp.abs(pos[:, None] - pos[None, :]))
    s = jnp.einsum('bnihd,bnjhd->bnhij', qc, kc) * intra
    o = jnp.einsum('bnhij,bnjhe->bnihe', s, vc)
    dec_end = jnp.exp(log_g[:, None] * (RET_CHUNK - 1.0 - pos)[None, :])
    dec_start = jnp.exp(log_g[:, None] * pos[None, :])
    kv_f = jnp.einsum('bnjhd,bnjhe,hj->nbhde', kc, vc, dec_end)
    kv_b = jnp.einsum('bnjhd,bnjhe,hj->nbhde', kc, vc, dec_start)
    g_chunk = jnp.exp(log_g * RET_CHUNK)[None, :, None, None]

    def chunk_step(state, kv):
        return g_chunk * state + kv, state

    s0 = jnp.zeros((bsz, RET_H, RET_DK, RET_DV), f32)
    _, s_prev = lax.scan(chunk_step, s0, kv_f)
    _, s_next = lax.scan(chunk_step, s0, kv_b, reverse=True)
    dq_f = jnp.exp(log_g[None, :] * (pos[:, None] + 1.0))[:, :, None]
    dq_b = jnp.exp(log_g[None, :] * (RET_CHUNK - pos[:, None]))[:, :, None]
    o = o + jnp.einsum('bnihd,nbhde->bnihe', qc * dq_f, s_prev) + jnp.einsum('bnihd,nbhde->bnihe', qc * dq_b, s_next)
    o = o.reshape(bsz, seqlen, RET_H, RET_DV)
    o = _head_norm(o, RET_GN_EPS) * gn_g.astype(f32).reshape(RET_H, RET_DV)
    return o.reshape(bsz, seqlen, W_BR)


def _s5_branch(u, lam_re, lam_im, log_step, b_re, b_im, c_re, c_im, d_skip, glu_w, glu_b):
    f32 = jnp.float32
    bsz, seqlen, _ = u.shape
    uf = u.astype(f32)
    b_mat = lax.complex(b_re.astype(f32), b_im.astype(f32))
    c_mat = lax.complex(c_re.astype(f32), c_im.astype(f32))
    bu = jnp.einsum('blgp,gnp->blgn', uf.reshape(bsz, seqlen, S5_G, S5_P).astype(jnp.complex64), b_mat)
    state = jnp.zeros(bu.shape, jnp.complex64)
    for d in range(2):
        lam = lax.complex(jnp.minimum(lam_re[d].astype(f32), LAM_RE_MAX), lam_im[d].astype(f32))
        step = jnp.exp(log_step[d].astype(f32))[:, None]
        lam_bar = jnp.exp(lam * step)
        b_gain = (lam_bar - 1.0) / lam
        a = jnp.broadcast_to(lam_bar, bu.shape)
        _, xs = lax.associative_scan(_linear_scan_combine, (a, b_gain * bu), reverse=(d == 1), axis=1)
        state = state + xs
    y = jnp.einsum('blgn,gpn->blgp', state, c_mat).real.reshape(bsz, seqlen, W_BR) + d_skip.astype(f32) * uf
    y = jax.nn.gelu(y)
    return y * jax.nn.sigmoid(y @ glu_w.astype(f32) + glu_b.astype(f32))


def _layer(x, norm_g, w_in, lru_conv_w, lru_conv_b, lru_w_r, lru_b_r, lru_w_i, lru_b_i, lru_lambda,
           rwkv_mu, rwkv_w0, rwkv_w2, rwkv_a0, rwkv_a2, rwkv_k_k, rwkv_k_a, rwkv_r_k, rwkv_lnx_g, rwkv_lnx_b,
           ret_gn_g, s5_lam_re, s5_lam_im, s5_log_step, s5_b_re, s5_b_im, s5_c_re, s5_c_im, s5_d, s5_glu_w,
           s5_glu_b, w_branch, w_out):
    f32 = jnp.float32
    dt = x.dtype
    bsz, seqlen, _ = x.shape
    h = _rmsnorm(x, norm_g)
    proj = h @ w_in
    (p_lru_x, p_lru_z, p_rw_r, p_rw_k, p_rw_v, p_rw_wdf, p_rw_wdb, p_rw_ad, p_rw_z,
     p_ret_q, p_ret_k, p_ret_v, p_ret_z, p_s5_u, p_s5_z, p_gates) = _split_cols(proj)
    silu = lambda z: jax.nn.silu(z.astype(f32))
    y_a = _rglru_branch(p_lru_x, lru_conv_w, lru_conv_b, lru_w_r, lru_b_r, lru_w_i, lru_b_i, lru_lambda) * silu(p_lru_z)
    y_b = _rwkv7_branch(p_rw_r, p_rw_k, p_rw_v, p_rw_wdf, p_rw_wdb, p_rw_ad, rwkv_mu, rwkv_w0, rwkv_w2, rwkv_a0,
                        rwkv_a2, rwkv_k_k, rwkv_k_a, rwkv_r_k, rwkv_lnx_g, rwkv_lnx_b) * silu(p_rw_z)
    y_c = _retention_branch(p_ret_q, p_ret_k, p_ret_v, ret_gn_g) * silu(p_ret_z)
    y_d = _s5_branch(p_s5_u, s5_lam_re, s5_lam_im, s5_log_step, s5_b_re, s5_b_im, s5_c_re, s5_c_im, s5_d,
                     s5_glu_w, s5_glu_b) * silu(p_s5_z)
    gates = jax.nn.sigmoid(p_gates.astype(f32)).reshape(bsz, seqlen, N_BRANCH, D_MODEL)
    merged = jnp.zeros((bsz, seqlen, D_MODEL), f32)
    for i, yb in enumerate((y_a, y_b, y_c, y_d)):
        merged = merged + gates[:, :, i] * (yb.astype(dt) @ w_branch[i]).astype(f32)
    out = merged.astype(dt) @ w_out
    return x + out


def _trunk(x, norm_g, w_in, lru_conv_w, lru_conv_b, lru_w_r, lru_b_r, lru_w_i, lru_b_i, lru_lambda,
           rwkv_mu, rwkv_w0, rwkv_w2, rwkv_a0, rwkv_a2, rwkv_k_k, rwkv_k_a, rwkv_r_k, rwkv_lnx_g, rwkv_lnx_b,
           ret_gn_g, s5_lam_re, s5_lam_im, s5_log_step, s5_b_re, s5_b_im, s5_c_re, s5_c_im, s5_d, s5_glu_w,
           s5_glu_b, w_branch, w_out, final_g):
    for l in range(DEPTH):
        x = _layer(x, norm_g[l], w_in[l], lru_conv_w[l], lru_conv_b[l], lru_w_r[l], lru_b_r[l], lru_w_i[l],
                   lru_b_i[l], lru_lambda[l], rwkv_mu[l], rwkv_w0[l], rwkv_w2[l], rwkv_a0[l], rwkv_a2[l],
                   rwkv_k_k[l], rwkv_k_a[l], rwkv_r_k[l], rwkv_lnx_g[l], rwkv_lnx_b[l], ret_gn_g[l],
                   s5_lam_re[l], s5_lam_im[l], s5_log_step[l], s5_b_re[l], s5_b_im[l], s5_c_re[l], s5_c_im[l],
                   s5_d[l], s5_glu_w[l], s5_glu_b[l], w_branch[l], w_out[l])
    return _rmsnorm(x, final_g)


def setup_inputs(seed: int = 0) -> dict:
    key = jax.random.key(seed)
    ks = jax.random.split(key, 40)
    f32 = jnp.float32
    nrm = lambda k, shape, scale: jax.random.normal(k, shape, f32) * scale
    a0 = jax.random.uniform(ks[9], (DEPTH, 2, W_BR), f32, minval=0.9, maxval=0.999)
    base = a0 ** (1.0 / LRU_C)
    lru_lambda = jnp.log(base) - jnp.log1p(-base)
    ratio = jnp.arange(W_BR, dtype=f32) / (W_BR - 1)
    w0_base = -7.0 + 5.0 * ratio ** 0.85 + 0.5
    return {
        'x_prompt': nrm(ks[0], (BATCH, SEQ, D_MODEL), 1.0),
        'x_sample': nrm(ks[1], (DEC_BATCH, DEC_SEQ, D_MODEL), 1.0),
        'norm_g': 1.0 + nrm(ks[2], (DEPTH, D_MODEL), 0.01),
        'w_in': nrm(ks[3], (DEPTH, D_MODEL, N_IN), D_MODEL ** -0.5),
        'lru_conv_w': nrm(ks[4], (DEPTH, CONV_W, W_BR), CONV_W ** -0.5),
        'lru_conv_b': nrm(ks[5], (DEPTH, W_BR), 0.01),
        'lru_w_r': nrm(ks[6], (DEPTH, 2, LRU_BLOCKS, LRU_BS, LRU_BS), LRU_BS ** -0.5),
        'lru_b_r': nrm(ks[7], (DEPTH, 2, W_BR), 0.01),
        'lru_w_i': nrm(ks[8], (DEPTH, 2, LRU_BLOCKS, LRU_BS, LRU_BS), LRU_BS ** -0.5),
        'lru_b_i': nrm(ks[10], (DEPTH, 2, W_BR), 0.01),
        'lru_lambda': lru_lambda,
        'rwkv_mu': jax.random.uniform(ks[11], (DEPTH, 3, W_BR), f32, minval=0.2, maxval=0.8),
        'rwkv_w0': w0_base + nrm(ks[12], (DEPTH, 2, W_BR), 0.1),
        'rwkv_w2': nrm(ks[13], (DEPTH, 2, RWKV_LORA_W, W_BR), 0.1 * RWKV_LORA_W ** -0.5),
        'rwkv_a0': nrm(ks[14], (DEPTH, W_BR), 0.1),
        'rwkv_a2': nrm(ks[15], (DEPTH, RWKV_LORA_A, W_BR), 0.1 * RWKV_LORA_A ** -0.5),
        'rwkv_k_k': 0.85 + nrm(ks[16], (DEPTH, W_BR), 0.05),
        'rwkv_k_a': 1.0 + nrm(ks[17], (DEPTH, W_BR), 0.05),
        'rwkv_r_k': -0.04 + nrm(ks[18], (DEPTH, RWKV_H, RWKV_HEAD), 0.1),
        'rwkv_lnx_g': 1.0 + nrm(ks[19], (DEPTH, W_BR), 0.01),
        'rwkv_lnx_b': nrm(ks[20], (DEPTH, W_BR), 0.01),
        'ret_gn_g': 1.0 + nrm(ks[21], (DEPTH, W_BR), 0.01),
        's5_lam_re': -0.5 + nrm(ks[22], (DEPTH, 2, S5_G, S5_N), 0.01),
        's5_lam_im': math.pi * jnp.arange(S5_N, dtype=f32) + nrm(ks[23], (DEPTH, 2, S5_G, S5_N), 0.01),
        's5_log_step': jax.random.uniform(ks[24], (DEPTH, 2, S5_G), f32, minval=math.log(0.001), maxval=math.log(0.1)),
        's5_b_re': nrm(ks[25], (DEPTH, S5_G, S5_N, S5_P), (2.0 * S5_P) ** -0.5),
        's5_b_im': nrm(ks[26], (DEPTH, S5_G, S5_N, S5_P), (2.0 * S5_P) ** -0.5),
        's5_c_re': nrm(ks[27], (DEPTH, S5_G, S5_P, S5_N), (2.0 * S5_N) ** -0.5),
        's5_c_im': nrm(ks[28], (DEPTH, S5_G, S5_P, S5_N), (2.0 * S5_N) ** -0.5),
        's5_d': nrm(ks[29], (DEPTH, W_BR), 1.0),
        's5_glu_w': nrm(ks[30], (DEPTH, W_BR, W_BR), W_BR ** -0.5),
        's5_glu_b': nrm(ks[31], (DEPTH, W_BR), 0.01),
        'w_branch': nrm(ks[32], (DEPTH, N_BRANCH, W_BR, D_MODEL), W_BR ** -0.5),
        'w_out': nrm(ks[33], (DEPTH, D_MODEL, D_MODEL), D_MODEL ** -0.5),
        'final_g': 1.0 + nrm(ks[34], (D_MODEL,), 0.01),
    }


def reference(x_prompt, x_sample, norm_g, w_in, lru_conv_w, lru_conv_b, lru_w_r, lru_b_r, lru_w_i, lru_b_i,
              lru_lambda, rwkv_mu, rwkv_w0, rwkv_w2, rwkv_a0, rwkv_a2, rwkv_k_k, rwkv_k_a, rwkv_r_k, rwkv_lnx_g,
              rwkv_lnx_b, ret_gn_g, s5_lam_re, s5_lam_im, s5_log_step, s5_b_re, s5_b_im, s5_c_re, s5_c_im, s5_d,
              s5_glu_w, s5_glu_b, w_branch, w_out, final_g):
    weights = (norm_g, w_in, lru_conv_w, lru_conv_b, lru_w_r, lru_b_r, lru_w_i, lru_b_i, lru_lambda,
               rwkv_mu, rwkv_w0, rwkv_w2, rwkv_a0, rwkv_a2, rwkv_k_k, rwkv_k_a, rwkv_r_k, rwkv_lnx_g, rwkv_lnx_b,
               ret_gn_g, s5_lam_re, s5_lam_im, s5_log_step, s5_b_re, s5_b_im, s5_c_re, s5_c_im, s5_d, s5_glu_w,
               s5_glu_b, w_branch, w_out, final_g)
    y_prompt = _trunk(x_prompt, *weights)
    y_sample = _trunk(x_sample, *weights)
    return (y_prompt, y_sample)
```

```python
import functools
import math

import numpy as np
import jax
import jax.numpy as jnp
from jax import lax
from jax.experimental import pallas as pl
from jax.experimental.pallas import tpu as pltpu

F32 = jnp.float32
BF16 = jnp.bfloat16

D_MODEL = 2048
DEPTH = 4
W_BR = D_MODEL // 2
N_BRANCH = 4
LRU_BLOCKS = 8
LRU_BS = W_BR // LRU_BLOCKS
LRU_C = 8.0
RWKV_HEAD = 64
RWKV_H = W_BR // RWKV_HEAD
RWKV_LORA = 64
RWKV_LN_EPS = RWKV_HEAD * 1e-5
RWKV_CHUNK = 64
RET_H = 4
RET_DK = W_BR // 2 // RET_H
RET_DV = W_BR // RET_H
RET_CHUNK = 128
RET_GN_EPS = 1e-5
ROPE_BASE = 10000.0
S5_P = 16
S5_G = W_BR // S5_P
S5_N = 64
S5_GB = 8
S5_NB = S5_G // S5_GB
S5_SW = S5_GB * S5_N
LAM_RE_MAX = -1e-4
NORM_EPS = 1e-6

SUBLANES = 8
LANES = 128
VMEM_LIMIT = 56 * 1024 * 1024

_ORIG_SPLITS = (
    ("lru_x", W_BR), ("lru_z", W_BR),
    ("rw_r", W_BR), ("rw_k", W_BR), ("rw_v", W_BR), ("rw_wdf", RWKV_LORA), ("rw_wdb", RWKV_LORA),
    ("rw_ad", RWKV_LORA), ("rw_z", W_BR),
    ("ret_q", RET_H * RET_DK), ("ret_k", RET_H * RET_DK), ("ret_v", W_BR), ("ret_z", W_BR),
    ("s5_u", W_BR), ("s5_z", W_BR),
    ("gates", N_BRANCH * D_MODEL),
)
_NEW_ORDER = ("lru_x", "lru_z", "rw_r", "rw_k", "rw_v", "rw_z", "ret_v", "ret_z", "s5_u", "s5_z", "gates",
              "ret_q", "ret_k", "rw_wdf", "rw_wdb", "rw_ad")


def _column_layout():
    orig, start = {}, 0
    for name, n in _ORIG_SPLITS:
        orig[name] = (start, n)
        start += n
    new, perm, pos = {}, [], 0
    for name in _NEW_ORDER:
        s, n = orig[name]
        new[name] = pos
        perm.extend(range(s, s + n))
        pos += n
    pad = (-pos) % 256
    return new, np.asarray(perm, np.int32), pos, pad


COL, _PERM, _N_IN, _N_PAD = _column_layout()
N_COLS = _N_IN + _N_PAD


def _cparams(sem):
    return pltpu.CompilerParams(dimension_semantics=sem, vmem_limit_bytes=VMEM_LIMIT)


def _sigmoid(x):
    return 1.0 / (1.0 + jnp.exp(-x))


def _softplus(x):
    return jnp.maximum(x, 0.0) + jnp.log1p(jnp.exp(-jnp.abs(x)))


def _one_minus_exp(x):
    series = -x * (1.0 + x * (0.5 + x * (1.0 / 6.0 + x * (1.0 / 24.0))))
    return jnp.where(x > -0.03, series, 1.0 - jnp.exp(x))


def _silu(x):
    return x * _sigmoid(x)


def _bdot(a, b):
    return jnp.dot(a.astype(BF16), b.astype(BF16), preferred_element_type=F32)


def _split_dot(x, w):
    hi = x.astype(BF16)
    lo = (x - hi.astype(F32)).astype(BF16)
    return (jnp.dot(hi, w, preferred_element_type=F32) + jnp.dot(lo, w, preferred_element_type=F32))


def _rmsnorm_rows(x, g):
    return x * lax.rsqrt(jnp.mean(x * x, axis=-1, keepdims=True) + NORM_EPS) * g


def _norm_call(x, g, tm):
    n, d = x.shape

    def body(x_ref, g_ref, o_ref):
        o_ref[...] = _rmsnorm_rows(x_ref[...], g_ref[...]).astype(BF16)

    return pl.pallas_call(
        body, grid=(n // tm,),
        in_specs=[pl.BlockSpec((tm, d), lambda i: (i, 0)), pl.BlockSpec((1, d), lambda i: (0, 0))],
        out_specs=pl.BlockSpec((tm, d), lambda i: (i, 0)),
        out_shape=jax.ShapeDtypeStruct((n, d), BF16),
        compiler_params=_cparams(("parallel",)), name="rmsnorm")(x, g)


def _inproj_call(h, w, tm, tn):
    n, k = h.shape
    nc = w.shape[1]

    def body(h_ref, w_ref, o_ref):
        o_ref[...] = jnp.dot(h_ref[...], w_ref[...], preferred_element_type=F32)

    return pl.pallas_call(
        body, grid=(n // tm, nc // tn),
        in_specs=[pl.BlockSpec((tm, k), lambda i, j: (i, 0)), pl.BlockSpec((k, tn), lambda i, j: (0, j))],
        out_specs=pl.BlockSpec((tm, tn), lambda i, j: (i, j)),
        out_shape=jax.ShapeDtypeStruct((n, nc), F32),
        compiler_params=_cparams(("parallel", "arbitrary")), name="inproj")(h, w)


def _outproj_call(x, merged, w_out, g, tm, last):
    n, d = x.shape

    def body(x_ref, m_ref, w_ref, g_ref, *o_refs):
        y = x_ref[...] + jnp.dot(m_ref[...], w_ref[...], preferred_element_type=F32)
        hn = _rmsnorm_rows(y, g_ref[...])
        if last:
            o_refs[0][...] = hn
        else:
            o_refs[0][...] = y
            o_refs[1][...] = hn.astype(BF16)

    row = pl.BlockSpec((tm, d), lambda i: (i, 0))
    if last:
        out_specs, out_shape = row, jax.ShapeDtypeStruct((n, d), F32)
    else:
        out_specs = (row, row)
        out_shape = (jax.ShapeDtypeStruct((n, d), F32), jax.ShapeDtypeStruct((n, d), BF16))
    return pl.pallas_call(
        body, grid=(n // tm,),
        in_specs=[row, row, pl.BlockSpec((d, d), lambda i: (0, 0)), pl.BlockSpec((1, d), lambda i: (0, 0))],
        out_specs=out_specs, out_shape=out_shape,
        compiler_params=_cparams(("parallel",)), name="outproj")(x, merged, w_out, g)


def _gelu_tanh(x):
    return 0.5 * x * (1.0 + jnp.tanh(math.sqrt(2.0 / math.pi) * (x + 0.044715 * (x * x * x))))


def _finalize_call(proj, lru_h, rw_y, rw_bonus, ret_o, s5_y, fprm, glu_w, ones_bd, tm):
    n = proj.shape[0]
    w = W_BR

    def body(lz_ref, rz_ref, cz_ref, su_ref, sz_ref, lh_ref, ry_ref, rb_ref, co_ref, sy_ref, p_ref, glu_ref,
             ones_ref, o_ref):
        o_ref[0] = ((lh_ref[0] + lh_ref[1]) * _silu(lz_ref[...])).astype(BF16)

        y = ry_ref[0] + ry_ref[1]
        ones = ones_ref[...]
        yc = y - _split_dot(y, ones) * (1.0 / RWKV_HEAD)
        var = _split_dot(yc * yc, ones) * (1.0 / RWKV_HEAD)
        yb = yc * lax.rsqrt(var + RWKV_LN_EPS) * p_ref[0:1, :] + p_ref[1:2, :] + rb_ref[...]
        o_ref[1] = (yb * _silu(rz_ref[...])).astype(BF16)

        for h in range(RET_H):
            sl = slice(RET_DV * h, RET_DV * (h + 1))
            oh = co_ref[0, :, sl] + co_ref[1, :, sl]
            oc = oh - jnp.mean(oh, axis=-1, keepdims=True)
            ov = jnp.mean(oc * oc, axis=-1, keepdims=True)
            yc_h = oc * lax.rsqrt(ov + RET_GN_EPS) * p_ref[2:3, sl]
            o_ref[2, :, sl] = (yc_h * _silu(cz_ref[:, sl])).astype(BF16)

        s = _gelu_tanh(sy_ref[0] + sy_ref[1] + p_ref[3:4, :] * su_ref[...])
        s = s * _sigmoid(_bdot(s, glu_ref[...]) + p_ref[4:5, :])
        o_ref[3] = (s * _silu(sz_ref[...])).astype(BF16)

    def col(name):
        cb = COL[name] // w
        return pl.BlockSpec((tm, w), lambda i: (i, cb))

    both = pl.BlockSpec((2, tm, w), lambda i: (0, i, 0))
    first = pl.BlockSpec((None, tm, w), lambda i: (0, i, 0))

    def const(shape):
        return pl.BlockSpec(tuple(shape), lambda i: (0,) * len(shape))

    return pl.pallas_call(
        body, grid=(n // tm,),
        in_specs=[col("lru_z"), col("rw_z"), col("ret_z"), col("s5_u"), col("s5_z"), both, both, first, both, both,
                  const(fprm.shape), const(glu_w.shape), const(ones_bd.shape)],
        out_specs=pl.BlockSpec((N_BRANCH, tm, w), lambda i: (0, i, 0)),
        out_shape=jax.ShapeDtypeStruct((N_BRANCH, n, w), BF16),
        compiler_params=_cparams(("parallel",)), name="finalize")(
            proj, proj, proj, proj, proj, lru_h, rw_y, rw_bonus, ret_o, s5_y, fprm, glu_w, ones_bd)


def _merge_call(ycat, proj, w_branch, tm, tn):
    n = proj.shape[0]
    gate0 = COL["gates"] // tn
    per_branch = D_MODEL // tn

    def body(y_ref, g0, g1, g2, g3, w_ref, o_ref):
        acc = None
        for b, g_ref in enumerate((g0, g1, g2, g3)):
            term = _sigmoid(g_ref[...]) * jnp.dot(y_ref[b], w_ref[b], preferred_element_type=F32)
            acc = term if acc is None else acc + term
        o_ref[...] = acc.astype(BF16)

    gates = [pl.BlockSpec((tm, tn), lambda i, j, b=b: (i, gate0 + b * per_branch + j)) for b in range(N_BRANCH)]
    return pl.pallas_call(
        body, grid=(n // tm, D_MODEL // tn),
        in_specs=[pl.BlockSpec((N_BRANCH, tm, W_BR), lambda i, j: (0, i, 0)), *gates,
                  pl.BlockSpec((N_BRANCH, W_BR, tn), lambda i, j: (0, 0, j))],
        out_specs=pl.BlockSpec((tm, tn), lambda i, j: (i, j)),
        out_shape=jax.ShapeDtypeStruct((n, D_MODEL), BF16),
        compiler_params=_cparams(("parallel", "arbitrary")), name="merge")(
            ycat, proj, proj, proj, proj, w_branch)


def _seq_flags(seq_lens, t):
    sf, ef = [], []
    for ln in seq_lens:
        assert ln % t == 0
        k = ln // t
        sf += [1] + [0] * (k - 1)
        ef += [0] * (k - 1) + [1]
    return jnp.asarray(sf, jnp.int32), jnp.asarray(ef, jnp.int32)


def _tile_of(d, i, n_t):
    return i + d * (n_t - 1 - 2 * i)


def _tile_spec(t, w, col, n_t):
    cb = col // w
    return pl.BlockSpec((t, w), lambda d, i, sf, ef: (_tile_of(d, i, n_t), cb))


def _halo_specs(t, w, col, n_t):
    cb = col // w
    hb = t // SUBLANES
    last = n_t * hb - 1
    prev = pl.BlockSpec((SUBLANES, w), lambda d, i, sf, ef: (jnp.maximum(_tile_of(d, i, n_t) * hb - 1, 0), cb))
    nxt = pl.BlockSpec((SUBLANES, w), lambda d, i, sf, ef: (jnp.minimum((_tile_of(d, i, n_t) + 1) * hb, last), cb))
    return prev, nxt


def _dir_spec(shape):
    nd = len(shape)
    return pl.BlockSpec((None,) + tuple(shape[1:]), lambda d, i, sf, ef: (d,) + (0,) * (nd - 1))


def _const_spec(shape):
    nd = len(shape)
    return pl.BlockSpec(tuple(shape), lambda d, i, sf, ef: (0,) * nd)


def _fill_halo(buf, x_ref, xp_ref, xn_ref, keep_p, keep_n, t):
    buf[0:SUBLANES, :] = xp_ref[...] * keep_p
    buf[SUBLANES:t + SUBLANES, :] = x_ref[...]
    buf[t + SUBLANES:t + 2 * SUBLANES, :] = xn_ref[...] * keep_n


def _seg_scan_real(a_ref, b_ref, o_ref, carry_ref, t, rev):
    s_len = t // SUBLANES
    nb = a_ref.shape[0]
    blocks = range(nb)

    def ld(ref, cb, s):
        return ref[cb, pl.ds(s, SUBLANES, stride=s_len), :]

    def step_of(j):
        return s_len - 1 - j if rev else j

    def pass1(j, hp):
        hs, ps = hp
        s = step_of(j)
        a = [ld(a_ref, cb, s) for cb in blocks]
        return (tuple(a[cb] * hs[cb] + ld(b_ref, cb, s) for cb in blocks),
                tuple(a[cb] * ps[cb] for cb in blocks))

    zero = jnp.zeros((SUBLANES, LANES), F32)
    es, ps = lax.fori_loop(0, s_len, pass1, ((zero,) * nb, (zero + 1.0,) * nb), unroll=2)
    starts = []
    for cb in blocks:
        c = carry_ref[cb, 0:1, :]
        rows = [None] * SUBLANES
        for k in (range(SUBLANES - 1, -1, -1) if rev else range(SUBLANES)):
            rows[k] = c
            c = ps[cb][k:k + 1, :] * c + es[cb][k:k + 1, :]
        carry_ref[cb, 0:1, :] = c
        starts.append(jnp.concatenate(rows, axis=0))

    def pass2(j, hs):
        s = step_of(j)
        out = []
        for cb in blocks:
            h = ld(a_ref, cb, s) * hs[cb] + ld(b_ref, cb, s)
            o_ref[cb, pl.ds(s, SUBLANES, stride=s_len), :] = h
            out.append(h)
        return tuple(out)

    lax.fori_loop(0, s_len, pass2, tuple(starts), unroll=2)


def _lru_call(proj, conv_w, conv_b, wcat, bcat, sp, flags, t):
    n = proj.shape[0]
    n_t = n // t
    w = W_BR

    def body(sf_ref, ef_ref, x_ref, xp_ref, xn_ref, cw_ref, cb_ref, w_ref, b_ref, sp_ref, o_ref,
             xbuf, a_scr, b_scr, h_scr, carry):
        d = pl.program_id(0)
        ti = _tile_of(d, pl.program_id(1), n_t)
        sf = sf_ref[ti]
        ef = ef_ref[ti]
        _fill_halo(xbuf, x_ref, xp_ref, xn_ref, (1 - sf).astype(F32), (1 - ef).astype(F32), t)
        o = SUBLANES
        xc = (cw_ref[0:1, :] * xbuf[o - 2:o - 2 + t, :] + cw_ref[1:2, :] * xbuf[o - 1:o - 1 + t, :]
              + cw_ref[2:3, :] * xbuf[o:o + t, :] + cw_ref[3:4, :] * xbuf[o + 1:o + 1 + t, :] + cb_ref[...])
        row = lax.broadcasted_iota(jnp.int32, (t, LRU_BS), 0)
        first_row = jnp.where(d == 0, 0, t - 1)
        at_seq_edge = jnp.where(d == 0, sf, ef) == 1
        first = jnp.logical_and(row == first_row, at_seq_edge)
        for hb in range(LRU_BLOCKS):
            sl = slice(LRU_BS * hb, LRU_BS * (hb + 1))
            xcb = xc[:, sl]
            g = _bdot(xcb, w_ref[hb])
            r = _sigmoid(g[:, :LRU_BS] + b_ref[0:1, sl])
            ig = _sigmoid(g[:, LRU_BS:] + b_ref[1:2, sl])
            log_a = (-LRU_C) * r * _softplus(-sp_ref[0:1, sl])
            mult = jnp.sqrt(_one_minus_exp(2.0 * log_a))
            mult = jnp.where(first, 1.0, mult)
            a_scr[hb] = jnp.exp(log_a)
            b_scr[hb] = mult * ig * xcb

        @pl.when(at_seq_edge)
        def _():
            carry[...] = jnp.zeros_like(carry)

        @pl.when(d == 0)
        def _():
            _seg_scan_real(a_scr, b_scr, h_scr, carry, t, False)

        @pl.when(d == 1)
        def _():
            _seg_scan_real(a_scr, b_scr, h_scr, carry, t, True)

        for hb in range(LRU_BLOCKS):
            o_ref[:, LRU_BS * hb:LRU_BS * (hb + 1)] = h_scr[hb]

    prev, nxt = _halo_specs(t, w, COL["lru_x"], n_t)
    grid_spec = pltpu.PrefetchScalarGridSpec(
        num_scalar_prefetch=2, grid=(2, n_t),
        in_specs=[_tile_spec(t, w, COL["lru_x"], n_t), prev, nxt,
                  _const_spec(conv_w.shape), _const_spec(conv_b.shape),
                  _dir_spec(wcat.shape), _dir_spec(bcat.shape), _dir_spec(sp.shape)],
        out_specs=pl.BlockSpec((None, t, w), lambda d, i, sf, ef: (d, _tile_of(d, i, n_t), 0)),
        scratch_shapes=[pltpu.VMEM((t + 2 * SUBLANES, w), F32)]
        + [pltpu.VMEM((LRU_BLOCKS, t, LRU_BS), F32)] * 3
        + [pltpu.VMEM((LRU_BLOCKS, SUBLANES, LRU_BS), F32)])
    return pl.pallas_call(
        body, grid_spec=grid_spec, out_shape=jax.ShapeDtypeStruct((2, n, w), F32),
        compiler_params=_cparams(("arbitrary", "arbitrary")), name="lru_sweep")(
            *flags, proj, proj, proj, conv_w, conv_b, wcat, bcat, sp)


def _prep_lru(conv_w, conv_b, w_r, b_r, w_i, b_i, lam):
    wcat = jnp.concatenate([w_r, w_i], axis=-1).astype(BF16)
    bcat = jnp.stack([b_r, b_i], axis=1)
    return conv_w, conv_b.reshape(1, W_BR), wcat, bcat, lam.reshape(2, 1, W_BR)


_NT3 = (((2,), (2,)), ((0,), (0,)))
_NN3 = (((2,), (1,)), ((0,), (0,)))


def _nt(a, b):
    return lax.dot_general(a.astype(BF16), b.astype(BF16), _NT3, preferred_element_type=F32)


def _nn(a, b):
    return lax.dot_general(a.astype(BF16), b.astype(BF16), _NN3, preferred_element_type=F32)


def _rwkv_call(proj, prm, w0, w2, a2, ones_bd, tri, flags, t):
    n = proj.shape[0]
    n_t = n // t
    w = W_BR
    c = RWKV_CHUNK
    n_c = t // c
    nh, hd = RWKV_H, RWKV_HEAD

    def body(sf_ref, ef_ref, r_ref, rp_ref, rn_ref, k_ref, kp_ref, kn_ref, v_ref, vp_ref, vn_ref, lora_ref,
             prm_ref, w0_ref, w2_ref, a2_ref, ones_ref, tri_ref, y_ref, bonus_ref,
             xbuf, at_s, rt_s, bt_s, kt_s, vt_s, yt_s, gam_s, state):
        d = pl.program_id(0)
        ti = _tile_of(d, pl.program_id(1), n_t)
        sf = sf_ref[ti]
        ef = ef_ref[ti]
        keep_p, keep_n = (1 - sf).astype(F32), (1 - ef).astype(F32)
        o = SUBLANES

        def mixed(x_ref, xp_ref, xn_ref, mu):
            _fill_halo(xbuf, x_ref, xp_ref, xn_ref, keep_p, keep_n, t)
            x = xbuf[o:o + t, :]
            return x + mu * (0.5 * (xbuf[o - 1:o - 1 + t, :] + xbuf[o + 1:o + 1 + t, :]) - x)

        r = mixed(r_ref, rp_ref, rn_ref, prm_ref[0:1, :])
        k = mixed(k_ref, kp_ref, kn_ref, prm_ref[1:2, :])
        v = mixed(v_ref, vp_ref, vn_ref, prm_ref[2:3, :])
        lora = lora_ref[...]
        wd = jnp.where(d == 0, lora[:, 0:RWKV_LORA], lora[:, RWKV_LORA:2 * RWKV_LORA])
        ad = lora[:, 2 * RWKV_LORA:3 * RWKV_LORA]
        a_icl = _sigmoid(prm_ref[3:4, :] + _bdot(ad, a2_ref[...]))
        w_log = -_softplus(-(w0_ref[...] + _bdot(jnp.tanh(wd), w2_ref[...]))) - 0.5
        logw = -jnp.exp(w_log)
        kkr = k * prm_ref[4:5, :]
        kk = kkr / jnp.maximum(jnp.sqrt(_split_dot(kkr * kkr, ones_ref[...])), 1e-12)
        k_mod = k * (1.0 + (a_icl - 1.0) * prm_ref[5:6, :])
        bonus_ref[...] = _split_dot(r * k_mod * prm_ref[6:7, :], ones_ref[...]) * v
        b_vec = kk * a_icl

        vt3 = v.T.reshape(nh, hd, t)
        tri_m = tri_ref[...]
        for cc in range(n_c):
            rows = slice(c * cc, c * (cc + 1))
            lw = logw[rows, :]
            hi = lw.astype(BF16)
            rem = lw - hi.astype(F32)
            mid = rem.astype(BF16)
            lo = (rem - mid.astype(F32)).astype(BF16)
            g = (jnp.dot(tri_m, hi, preferred_element_type=F32) + jnp.dot(tri_m, mid, preferred_element_type=F32)
                 + jnp.dot(tri_m, lo, preferred_element_type=F32))
            e_pos = jnp.exp(g)
            e_neg = jnp.exp(-g)
            at_c = -kk[rows, :] * jnp.exp(g - lw)
            rt_c = r[rows, :] * e_pos
            bt_c = b_vec[rows, :] * e_neg
            kt_c = k_mod[rows, :] * e_neg
            g_end = jnp.exp(jnp.where(d == 0, g[c - 1:c, :], g[0:1, :]))
            for h in range(nh):
                ls = slice(hd * h, hd * (h + 1))
                at_s[h, rows, :] = at_c[:, ls]
                rt_s[h, rows, :] = rt_c[:, ls]
                bt_s[h, rows, :] = bt_c[:, ls]
                kt_s[h, rows, :] = kt_c[:, ls]
                gam_s[h, SUBLANES * cc:SUBLANES * (cc + 1), :] = jnp.broadcast_to(g_end[:, ls], (SUBLANES, hd))
            vt_s[cc] = vt3[:, :, rows]

        @pl.when(jnp.where(d == 0, sf, ef) == 1)
        def _():
            state[...] = jnp.zeros_like(state)

        ii = lax.broadcasted_iota(jnp.int32, (c, c), 0)
        jj = lax.broadcasted_iota(jnp.int32, (c, c), 1)
        order = (1 - 2 * d) * (ii - jj)
        strict = (order > 0)[None]
        incl = (order >= 0)[None]
        eye = (ii == jj).astype(F32)[None]

        def chunk(j, carry):
            cc = jnp.where(d == 0, j, n_c - 1 - j)
            rows = pl.ds(pl.multiple_of(cc * c, c), c)
            at, rt, bt, kt = at_s[:, rows, :], rt_s[:, rows, :], bt_s[:, rows, :], kt_s[:, rows, :]
            vt = vt_s[cc]
            gam = gam_s[:, pl.ds(pl.multiple_of(cc * SUBLANES, SUBLANES), 1), :]
            s0 = state[...]
            a_ab = jnp.where(strict, _nt(at, bt), 0.0)
            a_ak = jnp.where(strict, _nt(at, kt), 0.0)
            a_rb = jnp.where(incl, _nt(rt, bt), 0.0)
            a_rk = jnp.where(incl, _nt(rt, kt), 0.0)
            tinv = eye + a_ab
            pw = a_ab
            for _ in range(int(math.log2(c)) - 1):
                pw = _nn(pw, pw)
                tinv = tinv + _nn(pw, tinv)
            p = _nn(tinv, at)
            qt = _nt(_nt(vt, a_ak), tinv)
            ut = _nt(s0, p) + qt
            yt_s[cc] = _nt(s0, rt) + _nt(ut, a_rb) + _nt(vt, a_rk)
            state[...] = (s0 + _nn(ut, bt) + _nn(vt, kt)) * gam
            return carry

        lax.fori_loop(0, n_c, chunk, 0)
        yt = jnp.concatenate([yt_s[cc] for cc in range(n_c)], axis=2)
        y_ref[...] = yt.reshape(w, t).T

    lora_col = COL["rw_wdf"]
    r_halo = _halo_specs(t, w, COL["rw_r"], n_t)
    k_halo = _halo_specs(t, w, COL["rw_k"], n_t)
    v_halo = _halo_specs(t, w, COL["rw_v"], n_t)
    head_major = pltpu.VMEM((nh, t, hd), F32)
    per_chunk = pltpu.VMEM((n_c, nh, hd, c), F32)
    grid_spec = pltpu.PrefetchScalarGridSpec(
        num_scalar_prefetch=2, grid=(2, n_t),
        in_specs=[_tile_spec(t, w, COL["rw_r"], n_t), *r_halo, _tile_spec(t, w, COL["rw_k"], n_t), *k_halo,
                  _tile_spec(t, w, COL["rw_v"], n_t), *v_halo, _tile_spec(t, 4 * RWKV_LORA, lora_col, n_t),
                  _const_spec(prm.shape), _dir_spec(w0.shape), _dir_spec(w2.shape), _const_spec(a2.shape),
                  _const_spec(ones_bd.shape), _dir_spec(tri.shape)],
        out_specs=(pl.BlockSpec((None, t, w), lambda d, i, sf, ef: (d, _tile_of(d, i, n_t), 0)),
                   pl.BlockSpec((None, t, w), lambda d, i, sf, ef: (d, _tile_of(d, i, n_t), 0))),
        scratch_shapes=[pltpu.VMEM((t + 2 * SUBLANES, w), F32), head_major, head_major, head_major, head_major,
                        per_chunk, per_chunk, pltpu.VMEM((nh, n_c * SUBLANES, hd), F32),
                        pltpu.VMEM((nh, hd, hd), F32)])
    return pl.pallas_call(
        body, grid_spec=grid_spec,
        out_shape=(jax.ShapeDtypeStruct((2, n, w), F32), jax.ShapeDtypeStruct((2, n, w), F32)),
        compiler_params=_cparams(("arbitrary", "arbitrary")), name="rwkv_sweep")(
            *flags, proj, proj, proj, proj, proj, proj, proj, proj, proj, proj, prm, w0, w2, a2, ones_bd, tri)


def _prep_rwkv(mu, w0, w2, a0, a2, k_k, k_a, r_k):
    prm = jnp.concatenate([mu, a0[None], k_k[None], k_a[None], r_k.reshape(1, W_BR),
                           jnp.zeros((1, W_BR), F32)], axis=0)
    head = np.arange(W_BR) // RWKV_HEAD
    ones_bd = jnp.asarray(head[:, None] == head[None, :], BF16)
    i = np.arange(RWKV_CHUNK)
    tri = jnp.asarray(np.stack([i[:, None] >= i[None, :], i[:, None] <= i[None, :]]), BF16)
    return prm, w0.reshape(2, 1, W_BR), w2.astype(BF16), a2.astype(BF16), ones_bd, tri


def _head_norm_ref(y, eps, hd):
    yh = y.reshape(y.shape[0], -1, hd)
    mean = jnp.mean(yh, axis=-1, keepdims=True)
    yc = yh - mean
    var = jnp.mean(yc * yc, axis=-1, keepdims=True)
    return (yc * lax.rsqrt(var + eps)).reshape(y.shape)


def _ret_tables(seq_lens):
    half = RET_DK // 2
    pos = jnp.concatenate([jnp.arange(ln, dtype=F32) for ln in seq_lens])
    inv = ROPE_BASE ** (-jnp.arange(half, dtype=F32) / half)
    ang = pos[:, None] * inv[None, :]
    cos, sin = jnp.cos(ang), jnp.sin(ang)
    cos_t = jnp.concatenate([cos, cos], axis=1)
    sin_t = jnp.concatenate([-sin, sin], axis=1)
    c = RET_CHUNK
    log_g = np.log(1.0 - 2.0 ** (-5.0 - np.arange(RET_H, dtype=np.float64)))
    i = np.arange(c, dtype=np.float64)
    dmat = np.exp(log_g[:, None, None] * np.abs(i[:, None] - i[None, :]))
    rows = np.stack([np.exp(log_g[:, None] * (i + 1.0)), np.exp(log_g[:, None] * (c - i)),
                     np.exp(log_g[:, None] * (c - 1.0 - i)), np.exp(log_g[:, None] * i)], axis=1)
    rows = np.broadcast_to(rows[..., None], (RET_H, 4, c, RET_DK))
    return cos_t, sin_t, jnp.asarray(dmat, F32), jnp.asarray(rows, F32)


def _ret_call(proj, cos_t, sin_t, dmat, rowsc, flags, t):
    n = proj.shape[0]
    n_t = n // t
    c = RET_CHUNK
    n_c = t // c
    g_chunk = [float((1.0 - 2.0 ** (-5.0 - h)) ** c) for h in range(RET_H)]
    tn_dims = (((0,), (0,)), ((), ()))
    nt_dims = (((1,), (1,)), ((), ()))

    def body(sf_ref, ef_ref, q_ref, k_ref, v_ref, cos_ref, sin_ref, dm_ref, rs_ref, o_ref, state):
        d = pl.program_id(0)
        ti = _tile_of(d, pl.program_id(1), n_t)
        at_seq_edge = jnp.where(d == 0, sf_ref[ti], ef_ref[ti]) == 1

        @pl.when(at_seq_edge)
        def _():
            state[...] = jnp.zeros_like(state)

        def rot(x_ref, rows, h):
            x = x_ref[rows, RET_DK * h:RET_DK * (h + 1)]
            return x * cos_ref[rows, :] + pltpu.roll(x, RET_DK // 2, 1) * sin_ref[rows, :]

        def chunk(cc, fwd):
            rows = slice(c * cc, c * (cc + 1))
            for h in range(RET_H):
                qh = rot(q_ref, rows, h)
                kh = rot(k_ref, rows, h) * (RET_DK ** -0.5)
                vh = v_ref[rows, RET_DV * h:RET_DV * (h + 1)].astype(BF16)
                s_old = state[h]
                if fwd:
                    sc = lax.dot_general(qh.astype(BF16), kh.astype(BF16), nt_dims,
                                         preferred_element_type=F32) * dm_ref[h]
                    o = (jnp.dot(sc.astype(BF16), vh, preferred_element_type=F32)
                         + _bdot(qh * rs_ref[h, 0], s_old))
                    kd = kh * rs_ref[h, 2]
                else:
                    o = _bdot(qh * rs_ref[h, 1], s_old)
                    kd = kh * rs_ref[h, 3]
                o_ref[rows, RET_DV * h:RET_DV * (h + 1)] = o
                state[h] = g_chunk[h] * s_old + lax.dot_general(kd.astype(BF16), vh, tn_dims,
                                                                 preferred_element_type=F32)

        @pl.when(d == 0)
        def _():
            for cc in range(n_c):
                chunk(cc, True)

        @pl.when(d == 1)
        def _():
            for cc in range(n_c - 1, -1, -1):
                chunk(cc, False)

    hk = RET_H * RET_DK
    tab = pl.BlockSpec((t, RET_DK), lambda d, i, sf, ef: (_tile_of(d, i, n_t), 0))
    grid_spec = pltpu.PrefetchScalarGridSpec(
        num_scalar_prefetch=2, grid=(2, n_t),
        in_specs=[_tile_spec(t, hk, COL["ret_q"], n_t), _tile_spec(t, hk, COL["ret_k"], n_t),
                  _tile_spec(t, W_BR, COL["ret_v"], n_t), tab, tab,
                  _const_spec(dmat.shape), _const_spec(rowsc.shape)],
        out_specs=pl.BlockSpec((None, t, W_BR), lambda d, i, sf, ef: (d, _tile_of(d, i, n_t), 0)),
        scratch_shapes=[pltpu.VMEM((RET_H, RET_DK, RET_DV), F32)])
    return pl.pallas_call(
        body, grid_spec=grid_spec, out_shape=jax.ShapeDtypeStruct((2, n, W_BR), F32),
        compiler_params=_cparams(("arbitrary", "arbitrary")), name="ret_sweep")(
            *flags, proj, proj, proj, cos_t, sin_t, dmat, rowsc)


def _seg_scan_complex(br_ref, bi_ref, sr_ref, si_ref, carry_r, carry_i, base, lam_ref, t, rev):
    s_len = t // SUBLANES
    nb = br_ref.shape[0]
    blocks = range(nb)
    lr = [jnp.broadcast_to(lam_ref[0:1, LANES * cb:LANES * (cb + 1)], (SUBLANES, LANES)) for cb in blocks]
    li = [jnp.broadcast_to(lam_ref[1:2, LANES * cb:LANES * (cb + 1)], (SUBLANES, LANES)) for cb in blocks]

    def ld(ref, cb, s):
        return ref[cb, pl.ds(s, SUBLANES, stride=s_len), :]

    def step_of(j):
        return s_len - 1 - j if rev else j

    def advance(cb, s, hr, hi):
        nr = lr[cb] * hr - li[cb] * hi + ld(br_ref, cb, s)
        ni = lr[cb] * hi + li[cb] * hr + ld(bi_ref, cb, s)
        return nr, ni

    def pass1(j, h):
        s = step_of(j)
        return tuple(advance(cb, s, *h[cb]) for cb in blocks)

    zero = jnp.zeros((SUBLANES, LANES), F32)
    ends = lax.fori_loop(0, s_len, pass1, ((zero, zero),) * nb, unroll=2)
    starts = []
    for cb in blocks:
        sl = slice(LANES * cb, LANES * (cb + 1))
        pr, pi = lam_ref[2:3, sl], lam_ref[3:4, sl]
        cr, ci = carry_r[base + cb, 0:1, :], carry_i[base + cb, 0:1, :]
        rows_r, rows_i = [None] * SUBLANES, [None] * SUBLANES
        for k in (range(SUBLANES - 1, -1, -1) if rev else range(SUBLANES)):
            rows_r[k], rows_i[k] = cr, ci
            er, ei = ends[cb][0][k:k + 1, :], ends[cb][1][k:k + 1, :]
            cr, ci = pr * cr - pi * ci + er, pr * ci + pi * cr + ei
        carry_r[base + cb, 0:1, :] = cr
        carry_i[base + cb, 0:1, :] = ci
        starts.append((jnp.concatenate(rows_r, axis=0), jnp.concatenate(rows_i, axis=0)))

    def pass2(j, h):
        s = step_of(j)
        out = []
        for cb in blocks:
            nr, ni = advance(cb, s, *h[cb])
            sr_ref[cb, pl.ds(s, SUBLANES, stride=s_len), :] = nr
            si_ref[cb, pl.ds(s, SUBLANES, stride=s_len), :] = ni
            out.append((nr, ni))
        return tuple(out)

    lax.fori_loop(0, s_len, pass2, tuple(starts), unroll=2)


def _s5_call(proj, bblk, cblk, lam4, flags, t):
    n = proj.shape[0]
    n_t = n // t
    w = W_BR
    lb = S5_SW // LANES

    def body(sf_ref, ef_ref, u_ref, b_ref, c_ref, lam_ref, o_ref, br, bi, sr, si, carry_r, carry_i):
        d = pl.program_id(0)
        ti = _tile_of(d, pl.program_id(1), n_t)
        at_seq_edge = jnp.where(d == 0, sf_ref[ti], ef_ref[ti]) == 1

        @pl.when(at_seq_edge)
        def _():
            carry_r[...] = jnp.zeros_like(carry_r)
            carry_i[...] = jnp.zeros_like(carry_i)

        for ob in range(S5_NB):
            bu = _bdot(u_ref[:, LANES * ob:LANES * (ob + 1)], b_ref[ob])
            for cb in range(lb):
                br[cb] = bu[:, LANES * cb:LANES * (cb + 1)]
                bi[cb] = bu[:, S5_SW + LANES * cb:S5_SW + LANES * (cb + 1)]

            @pl.when(d == 0)
            def _():
                _seg_scan_complex(br, bi, sr, si, carry_r, carry_i, ob * lb, lam_ref.at[ob], t, False)

            @pl.when(d == 1)
            def _():
                _seg_scan_complex(br, bi, sr, si, carry_r, carry_i, ob * lb, lam_ref.at[ob], t, True)

            st = jnp.concatenate([sr[cb] for cb in range(lb)] + [si[cb] for cb in range(lb)], axis=1)
            o_ref[:, LANES * ob:LANES * (ob + 1)] = _bdot(st, c_ref[ob])

    grid_spec = pltpu.PrefetchScalarGridSpec(
        num_scalar_prefetch=2, grid=(2, n_t),
        in_specs=[_tile_spec(t, w, COL["s5_u"], n_t), _const_spec(bblk.shape), _dir_spec(cblk.shape),
                  _dir_spec(lam4.shape)],
        out_specs=pl.BlockSpec((None, t, w), lambda d, i, sf, ef: (d, _tile_of(d, i, n_t), 0)),
        scratch_shapes=[pltpu.VMEM((lb, t, LANES), F32)] * 4
        + [pltpu.VMEM((S5_NB * lb, SUBLANES, LANES), F32)] * 2)
    return pl.pallas_call(
        body, grid_spec=grid_spec, out_shape=jax.ShapeDtypeStruct((2, n, w), F32),
        compiler_params=_cparams(("arbitrary", "arbitrary")), name="s5_sweep")(
            *flags, proj, bblk, cblk, lam4)


def _prep_s5(lam_re, lam_im, log_step, b_re, b_im, c_re, c_im, t):
    s_len = t // SUBLANES
    assert s_len & (s_len - 1) == 0
    lre = jnp.minimum(lam_re, LAM_RE_MAX)
    step = jnp.exp(log_step)[..., None]
    mag = jnp.exp(lre * step)
    lbr, lbi = mag * jnp.cos(lam_im * step), mag * jnp.sin(lam_im * step)
    den = lre * lre + lam_im * lam_im
    gr = ((lbr - 1.0) * lre + lbi * lam_im) / den
    gi = (lbi * lre - (lbr - 1.0) * lam_im) / den
    pr, pi = lbr, lbi
    for _ in range(int(math.log2(s_len))):
        pr, pi = pr * pr - pi * pi, 2.0 * pr * pi
    lam4 = jnp.stack([x.reshape(2, S5_NB, S5_SW) for x in (lbr, lbi, pr, pi)], axis=2)
    cpr = c_re[None] * gr[:, :, None, :] - c_im[None] * gi[:, :, None, :]
    cpi = c_re[None] * gi[:, :, None, :] + c_im[None] * gr[:, :, None, :]
    eye = jnp.eye(S5_GB, dtype=F32)

    def c_rows(x):
        x = x.reshape(2, S5_NB, S5_GB, S5_P, S5_N)
        return jnp.einsum('dogpn,gh->dognhp', x, eye).reshape(2, S5_NB, S5_SW, S5_GB * S5_P)

    cblk = jnp.concatenate([c_rows(cpr), c_rows(-cpi)], axis=2).astype(BF16)

    def b_cols(x):
        x = x.reshape(S5_NB, S5_GB, S5_N, S5_P)
        return jnp.einsum('ognp,gh->ogphn', x, eye).reshape(S5_NB, S5_GB * S5_P, S5_SW)

    bblk = jnp.concatenate([b_cols(b_re), b_cols(b_im)], axis=2).astype(BF16)
    return bblk, cblk, lam4


def _tile_plan(n, seq_lens):
    shortest = min(seq_lens)
    return dict(
        sweep=min(512, shortest),
        rwkv=min(256, shortest),
        in_m=min(1024, n), in_n=1792,
        fin_m=min(256, n),
        merge_m=min(512, n), merge_n=512,
        out_m=min(256, n),
        norm_m=min(512, n),
    )


def _permute_w_in(w_in):
    orig, start = {}, 0
    for name, nn in _ORIG_SPLITS:
        orig[name] = (start, nn)
        start += nn
    parts = [w_in[:, orig[name][0]:orig[name][0] + orig[name][1]] for name in _NEW_ORDER]
    parts.append(jnp.zeros((w_in.shape[0], _N_PAD), w_in.dtype))
    return jnp.concatenate(parts, axis=1).astype(BF16)


def _trunk(x, seq_lens, norm_g, w_in, lru_conv_w, lru_conv_b, lru_w_r, lru_b_r, lru_w_i, lru_b_i, lru_lambda,
           rwkv_mu, rwkv_w0, rwkv_w2, rwkv_a0, rwkv_a2, rwkv_k_k, rwkv_k_a, rwkv_r_k, rwkv_lnx_g, rwkv_lnx_b,
           ret_gn_g, s5_lam_re, s5_lam_im, s5_log_step, s5_b_re, s5_b_im, s5_c_re, s5_c_im, s5_d, s5_glu_w,
           s5_glu_b, w_branch, w_out, final_g):
    n = x.shape[0]
    tp = _tile_plan(n, seq_lens)
    flags = _seq_flags(seq_lens, tp["sweep"])
    flags_rw = _seq_flags(seq_lens, tp["rwkv"])
    ret_tabs = _ret_tables(seq_lens)
    depth = w_in.shape[0]
    h = _norm_call(x, norm_g[0].reshape(1, D_MODEL), tp["norm_m"])
    for l in range(depth):
        proj = _inproj_call(h, _permute_w_in(w_in[l]), tp["in_m"], tp["in_n"])
        lru_h = _lru_call(proj, *_prep_lru(lru_conv_w[l], lru_conv_b[l], lru_w_r[l], lru_b_r[l], lru_w_i[l],
                                           lru_b_i[l], lru_lambda[l]), flags, tp["sweep"])
        rw_prm = _prep_rwkv(rwkv_mu[l], rwkv_w0[l], rwkv_w2[l], rwkv_a0[l], rwkv_a2[l], rwkv_k_k[l], rwkv_k_a[l],
                            rwkv_r_k[l])
        rw_y, rw_bonus = _rwkv_call(proj, *rw_prm, flags_rw, tp["rwkv"])
        ret_o = _ret_call(proj, *ret_tabs, flags, tp["sweep"])
        s5_y = _s5_call(proj, *_prep_s5(s5_lam_re[l], s5_lam_im[l], s5_log_step[l], s5_b_re[l], s5_b_im[l],
                                        s5_c_re[l], s5_c_im[l], tp["sweep"]), flags, tp["sweep"])
        fprm = jnp.stack([rwkv_lnx_g[l], rwkv_lnx_b[l], ret_gn_g[l], s5_d[l], s5_glu_b[l],
                          jnp.zeros_like(s5_d[l]), jnp.zeros_like(s5_d[l]), jnp.zeros_like(s5_d[l])])
        ycat = _finalize_call(proj, lru_h, rw_y, rw_bonus, ret_o, s5_y, fprm, s5_glu_w[l].astype(BF16),
                              rw_prm[4], tp["fin_m"])
        merged = _merge_call(ycat, proj, w_branch[l].astype(BF16), tp["merge_m"], tp["merge_n"])
        last = l == depth - 1
        g_next = (final_g if last else norm_g[l + 1]).reshape(1, D_MODEL)
        res = _outproj_call(x, merged, w_out[l].astype(BF16), g_next, tp["out_m"], last)
        if last:
            return res
        x, h = res


def kernel(x_prompt, x_sample, norm_g, w_in, lru_conv_w, lru_conv_b, lru_w_r, lru_b_r, lru_w_i, lru_b_i, lru_lambda, rwkv_mu, rwkv_w0, rwkv_w2, rwkv_a0, rwkv_a2, rwkv_k_k, rwkv_k_a, rwkv_r_k, rwkv_lnx_g, rwkv_lnx_b, ret_gn_g, s5_lam_re, s5_lam_im, s5_log_step, s5_b_re, s5_b_im, s5_c_re, s5_c_im, s5_d, s5_glu_w, s5_glu_b, w_branch, w_out, final_g):
    bp, lp, d = x_prompt.shape
    bs, ls, _ = x_sample.shape
    seq_lens = (lp,) * bp + (ls,) * bs
    x = jnp.concatenate([x_prompt.reshape(bp * lp, d), x_sample.reshape(bs * ls, d)], axis=0)
    y = _trunk(x, seq_lens, norm_g, w_in, lru_conv_w, lru_conv_b, lru_w_r, lru_b_r, lru_w_i, lru_b_i, lru_lambda,
               rwkv_mu, rwkv_w0, rwkv_w2, rwkv_a0, rwkv_a2, rwkv_k_k, rwkv_k_a, rwkv_r_k, rwkv_lnx_g, rwkv_lnx_b,
               ret_gn_g, s5_lam_re, s5_lam_im, s5_log_step, s5_b_re, s5_b_im, s5_c_re, s5_c_im, s5_d, s5_glu_w,
               s5_glu_b, w_branch, w_out, final_g)
    return y[:bp * lp].reshape(bp, lp, d), y[bp * lp:].reshape(bs, ls, d)
```

```python
import functools
import math

import numpy as np
import jax
import jax.numpy as jnp
from jax import lax
from jax.experimental import pallas as pl
from jax.experimental.pallas import tpu as pltpu

F32 = jnp.float32
BF16 = jnp.bfloat16

D_MODEL = 2048
DEPTH = 4
W_BR = D_MODEL // 2
N_BRANCH = 4
LRU_BLOCKS = 8
LRU_BS = W_BR // LRU_BLOCKS
LRU_C = 8.0
RWKV_HEAD = 64
RWKV_H = W_BR // RWKV_HEAD
RWKV_LORA = 64
RWKV_LN_EPS = RWKV_HEAD * 1e-5
RWKV_CHUNK = 64
RET_H = 4
RET_DK = W_BR // 2 // RET_H
RET_DV = W_BR // RET_H
RET_CHUNK = 128
RET_GN_EPS = 1e-5
ROPE_BASE = 10000.0
S5_P = 16
S5_G = W_BR // S5_P
S5_N = 64
S5_GB = 8
S5_NB = S5_G // S5_GB
S5_SW = S5_GB * S5_N
LAM_RE_MAX = -1e-4
NORM_EPS = 1e-6

SUBLANES = 8
LANES = 128
VMEM_LIMIT = 56 * 1024 * 1024

_ORIG_SPLITS = (
    ("lru_x", W_BR), ("lru_z", W_BR),
    ("rw_r", W_BR), ("rw_k", W_BR), ("rw_v", W_BR), ("rw_wdf", RWKV_LORA), ("rw_wdb", RWKV_LORA),
    ("rw_ad", RWKV_LORA), ("rw_z", W_BR),
    ("ret_q", RET_H * RET_DK), ("ret_k", RET_H * RET_DK), ("ret_v", W_BR), ("ret_z", W_BR),
    ("s5_u", W_BR), ("s5_z", W_BR),
    ("gates", N_BRANCH * D_MODEL),
)
_NEW_ORDER = ("lru_x", "lru_z", "rw_r", "rw_k", "rw_v", "rw_z", "ret_v", "ret_z", "s5_u", "s5_z", "gates",
              "ret_q", "ret_k", "rw_wdf", "rw_wdb", "rw_ad")


def _column_layout():
    orig, start = {}, 0
    for name, n in _ORIG_SPLITS:
        orig[name] = (start, n)
        start += n
    new, perm, pos = {}, [], 0
    for name in _NEW_ORDER:
        s, n = orig[name]
        new[name] = pos
        perm.extend(range(s, s + n))
        pos += n
    pad = (-pos) % 256
    return new, np.asarray(perm, np.int32), pos, pad


COL, _PERM, _N_IN, _N_PAD = _column_layout()
N_COLS = _N_IN + _N_PAD


def _cparams(sem):
    return pltpu.CompilerParams(dimension_semantics=sem, vmem_limit_bytes=VMEM_LIMIT)


def _sigmoid(x):
    return 1.0 / (1.0 + jnp.exp(-x))


def _softplus(x):
    return jnp.maximum(x, 0.0) + jnp.log1p(jnp.exp(-jnp.abs(x)))


def _one_minus_exp(x):
    series = -x * (1.0 + x * (0.5 + x * (1.0 / 6.0 + x * (1.0 / 24.0))))
    return jnp.where(x > -0.03, series, 1.0 - jnp.exp(x))


def _silu(x):
    return x * _sigmoid(x)


def _bdot(a, b):
    return jnp.dot(a.astype(BF16), b.astype(BF16), preferred_element_type=F32)


def _split_dot(x, w):
    hi = x.astype(BF16)
    lo = (x - hi.astype(F32)).astype(BF16)
    return (jnp.dot(hi, w, preferred_element_type=F32) + jnp.dot(lo, w, preferred_element_type=F32))


def _rmsnorm_rows(x, g):
    return x * lax.rsqrt(jnp.mean(x * x, axis=-1, keepdims=True) + NORM_EPS) * g


def _norm_call(x, g, tm):
    n, d = x.shape

    def body(x_ref, g_ref, o_ref):
        o_ref[...] = _rmsnorm_rows(x_ref[...], g_ref[...]).astype(BF16)

    return pl.pallas_call(
        body, grid=(n // tm,),
        in_specs=[pl.BlockSpec((tm, d), lambda i: (i, 0)), pl.BlockSpec((1, d), lambda i: (0, 0))],
        out_specs=pl.BlockSpec((tm, d), lambda i: (i, 0)),
        out_shape=jax.ShapeDtypeStruct((n, d), BF16),
        compiler_params=_cparams(("parallel",)), name="rmsnorm")(x, g)


def _inproj_call(h, w, tm, tn):
    n, k = h.shape
    nc = w.shape[1]

    def body(h_ref, w_ref, o_ref):
        o_ref[...] = jnp.dot(h_ref[...], w_ref[...], preferred_element_type=F32)

    return pl.pallas_call(
        body, grid=(n // tm, nc // tn),
        in_specs=[pl.BlockSpec((tm, k), lambda i, j: (i, 0)), pl.BlockSpec((k, tn), lambda i, j: (0, j))],
        out_specs=pl.BlockSpec((tm, tn), lambda i, j: (i, j)),
        out_shape=jax.ShapeDtypeStruct((n, nc), F32),
        compiler_params=_cparams(("parallel", "arbitrary")), name="inproj")(h, w)


def _outproj_call(x, merged, w_out, g, tm, last):
    n, d = x.shape

    def body(x_ref, m_ref, w_ref, g_ref, *o_refs):
        y = x_ref[...] + jnp.dot(m_ref[...], w_ref[...], preferred_element_type=F32)
        hn = _rmsnorm_rows(y, g_ref[...])
        if last:
            o_refs[0][...] = hn
        else:
            o_refs[0][...] = y
            o_refs[1][...] = hn.astype(BF16)

    row = pl.BlockSpec((tm, d), lambda i: (i, 0))
    if last:
        out_specs, out_shape = row, jax.ShapeDtypeStruct((n, d), F32)
    else:
        out_specs = (row, row)
        out_shape = (jax.ShapeDtypeStruct((n, d), F32), jax.ShapeDtypeStruct((n, d), BF16))
    return pl.pallas_call(
        body, grid=(n // tm,),
        in_specs=[row, row, pl.BlockSpec((d, d), lambda i: (0, 0)), pl.BlockSpec((1, d), lambda i: (0, 0))],
        out_specs=out_specs, out_shape=out_shape,
        compiler_params=_cparams(("parallel",)), name="outproj")(x, merged, w_out, g)


def _gelu_tanh(x):
    return 0.5 * x * (1.0 + jnp.tanh(math.sqrt(2.0 / math.pi) * (x + 0.044715 * (x * x * x))))


def _finalize_call(proj, lru_h, rw_y, rw_bonus, ret_o, s5_y, fprm, glu_w, ones_bd, tm):
    n = proj.shape[0]
    w = W_BR

    def body(lz_ref, rz_ref, cz_ref, su_ref, sz_ref, lh_ref, ry_ref, rb_ref, co_ref, sy_ref, p_ref, glu_ref,
             ones_ref, o_ref):
        o_ref[0] = ((lh_ref[0] + lh_ref[1]) * _silu(lz_ref[...])).astype(BF16)

        y = ry_ref[0] + ry_ref[1]
        ones = ones_ref[...]
        yc = y - _split_dot(y, ones) * (1.0 / RWKV_HEAD)
        var = _split_dot(yc * yc, ones) * (1.0 / RWKV_HEAD)
        yb = yc * lax.rsqrt(var + RWKV_LN_EPS) * p_ref[0:1, :] + p_ref[1:2, :] + rb_ref[...]
        o_ref[1] = (yb * _silu(rz_ref[...])).astype(BF16)

        for h in range(RET_H):
            sl = slice(RET_DV * h, RET_DV * (h + 1))
            oh = co_ref[0, :, sl] + co_ref[1, :, sl]
            oc = oh - jnp.mean(oh, axis=-1, keepdims=True)
            ov = jnp.mean(oc * oc, axis=-1, keepdims=True)
            yc_h = oc * lax.rsqrt(ov + RET_GN_EPS) * p_ref[2:3, sl]
            o_ref[2, :, sl] = (yc_h * _silu(cz_ref[:, sl])).astype(BF16)

        s = _gelu_tanh(sy_ref[0] + sy_ref[1] + p_ref[3:4, :] * su_ref[...])
        s = s * _sigmoid(_bdot(s, glu_ref[...]) + p_ref[4:5, :])
        o_ref[3] = (s * _silu(sz_ref[...])).astype(BF16)

    def col(name):
        cb = COL[name] // w
        return pl.BlockSpec((tm, w), lambda i: (i, cb))

    both = pl.BlockSpec((2, tm, w), lambda i: (0, i, 0))
    first = pl.BlockSpec((None, tm, w), lambda i: (0, i, 0))

    def const(shape):
        return pl.BlockSpec(tuple(shape), lambda i: (0,) * len(shape))

    return pl.pallas_call(
        body, grid=(n // tm,),
        in_specs=[col("lru_z"), col("rw_z"), col("ret_z"), col("s5_u"), col("s5_z"), both, both, first, both, both,
                  const(fprm.shape), const(glu_w.shape), const(ones_bd.shape)],
        out_specs=pl.BlockSpec((N_BRANCH, tm, w), lambda i: (0, i, 0)),
        out_shape=jax.ShapeDtypeStruct((N_BRANCH, n, w), BF16),
        compiler_params=_cparams(("parallel",)), name="finalize")(
            proj, proj, proj, proj, proj, lru_h, rw_y, rw_bonus, ret_o, s5_y, fprm, glu_w, ones_bd)


def _merge_call(ycat, proj, w_branch, tm, tn):
    n = proj.shape[0]
    gate0 = COL["gates"] // tn
    per_branch = D_MODEL // tn

    def body(y_ref, g0, g1, g2, g3, w_ref, o_ref):
        acc = None
        for b, g_ref in enumerate((g0, g1, g2, g3)):
            term = _sigmoid(g_ref[...]) * jnp.dot(y_ref[b], w_ref[b], preferred_element_type=F32)
            acc = term if acc is None else acc + term
        o_ref[...] = acc.astype(BF16)

    gates = [pl.BlockSpec((tm, tn), lambda i, j, b=b: (i, gate0 + b * per_branch + j)) for b in range(N_BRANCH)]
    return pl.pallas_call(
        body, grid=(n // tm, D_MODEL // tn),
        in_specs=[pl.BlockSpec((N_BRANCH, tm, W_BR), lambda i, j: (0, i, 0)), *gates,
                  pl.BlockSpec((N_BRANCH, W_BR, tn), lambda i, j: (0, 0, j))],
        out_specs=pl.BlockSpec((tm, tn), lambda i, j: (i, j)),
        out_shape=jax.ShapeDtypeStruct((n, D_MODEL), BF16),
        compiler_params=_cparams(("parallel", "arbitrary")), name="merge")(
            ycat, proj, proj, proj, proj, w_branch)


def _seq_flags(seq_lens, t):
    sf, ef = [], []
    for ln in seq_lens:
        assert ln % t == 0
        k = ln // t
        sf += [1] + [0] * (k - 1)
        ef += [0] * (k - 1) + [1]
    return jnp.asarray(sf, jnp.int32), jnp.asarray(ef, jnp.int32)


def _tile_of(d, i, n_t):
    return i + d * (n_t - 1 - 2 * i)


def _tile_spec(t, w, col, n_t):
    cb = col // w
    return pl.BlockSpec((t, w), lambda d, i, sf, ef: (_tile_of(d, i, n_t), cb))


def _halo_specs(t, w, col, n_t):
    cb = col // w
    hb = t // SUBLANES
    last = n_t * hb - 1
    prev = pl.BlockSpec((SUBLANES, w), lambda d, i, sf, ef: (jnp.maximum(_tile_of(d, i, n_t) * hb - 1, 0), cb))
    nxt = pl.BlockSpec((SUBLANES, w), lambda d, i, sf, ef: (jnp.minimum((_tile_of(d, i, n_t) + 1) * hb, last), cb))
    return prev, nxt


def _dir_spec(shape):
    nd = len(shape)
    return pl.BlockSpec((None,) + tuple(shape[1:]), lambda d, i, sf, ef: (d,) + (0,) * (nd - 1))


def _const_spec(shape):
    nd = len(shape)
    return pl.BlockSpec(tuple(shape), lambda d, i, sf, ef: (0,) * nd)


def _fill_halo(buf, x_ref, xp_ref, xn_ref, keep_p, keep_n, t):
    buf[0:SUBLANES, :] = xp_ref[...] * keep_p
    buf[SUBLANES:t + SUBLANES, :] = x_ref[...]
    buf[t + SUBLANES:t + 2 * SUBLANES, :] = xn_ref[...] * keep_n


def _seg_pitch(t):
    return t // SUBLANES + 4


def _seg_rows(t):
    return SUBLANES * _seg_pitch(t)


def _to_segments(ref, blk, val, t):
    s_len, pitch = t // SUBLANES, _seg_pitch(t)
    for k in range(SUBLANES):
        ref[blk, pitch * k:pitch * k + s_len, :] = val[s_len * k:s_len * (k + 1), :]


def _from_segments(ref, blk, t):
    s_len, pitch = t // SUBLANES, _seg_pitch(t)
    return jnp.concatenate([ref[blk, pitch * k:pitch * k + s_len, :] for k in range(SUBLANES)], axis=0)


def _seg_scan_real(a_ref, b_ref, o_ref, carry_ref, t, rev):
    s_len = t // SUBLANES
    pitch = _seg_pitch(t)
    nb = a_ref.shape[0]
    blocks = range(nb)

    def ld(ref, cb, s):
        return ref[cb, pl.ds(s, SUBLANES, stride=pitch), :]

    def step_of(j):
        return s_len - 1 - j if rev else j

    def pass1(j, hp):
        hs, ps = hp
        s = step_of(j)
        a = [ld(a_ref, cb, s) for cb in blocks]
        return (tuple(a[cb] * hs[cb] + ld(b_ref, cb, s) for cb in blocks),
                tuple(a[cb] * ps[cb] for cb in blocks))

    zero = jnp.zeros((SUBLANES, LANES), F32)
    es, ps = lax.fori_loop(0, s_len, pass1, ((zero,) * nb, (zero + 1.0,) * nb), unroll=2)
    starts = []
    for cb in blocks:
        c = carry_ref[cb, 0:1, :]
        rows = [None] * SUBLANES
        for k in (range(SUBLANES - 1, -1, -1) if rev else range(SUBLANES)):
            rows[k] = c
            c = ps[cb][k:k + 1, :] * c + es[cb][k:k + 1, :]
        carry_ref[cb, 0:1, :] = c
        starts.append(jnp.concatenate(rows, axis=0))

    def pass2(j, hs):
        s = step_of(j)
        out = []
        for cb in blocks:
            h = ld(a_ref, cb, s) * hs[cb] + ld(b_ref, cb, s)
            o_ref[cb, pl.ds(s, SUBLANES, stride=pitch), :] = h
            out.append(h)
        return tuple(out)

    lax.fori_loop(0, s_len, pass2, tuple(starts), unroll=2)


def _lru_call(proj, conv_w, conv_b, wcat, bcat, sp, flags, t):
    n = proj.shape[0]
    n_t = n // t
    w = W_BR

    def body(sf_ref, ef_ref, x_ref, xp_ref, xn_ref, cw_ref, cb_ref, w_ref, b_ref, sp_ref, o_ref,
             xbuf, a_scr, b_scr, h_scr, carry):
        d = pl.program_id(0)
        ti = _tile_of(d, pl.program_id(1), n_t)
        sf = sf_ref[ti]
        ef = ef_ref[ti]
        _fill_halo(xbuf, x_ref, xp_ref, xn_ref, (1 - sf).astype(F32), (1 - ef).astype(F32), t)
        o = SUBLANES
        xc = (cw_ref[0:1, :] * xbuf[o - 2:o - 2 + t, :] + cw_ref[1:2, :] * xbuf[o - 1:o - 1 + t, :]
              + cw_ref[2:3, :] * xbuf[o:o + t, :] + cw_ref[3:4, :] * xbuf[o + 1:o + 1 + t, :] + cb_ref[...])
        row = lax.broadcasted_iota(jnp.int32, (t, LRU_BS), 0)
        first_row = jnp.where(d == 0, 0, t - 1)
        at_seq_edge = jnp.where(d == 0, sf, ef) == 1
        first = jnp.logical_and(row == first_row, at_seq_edge)
        for hb in range(LRU_BLOCKS):
            sl = slice(LRU_BS * hb, LRU_BS * (hb + 1))
            xcb = xc[:, sl]
            g = _bdot(xcb, w_ref[hb])
            r = _sigmoid(g[:, :LRU_BS] + b_ref[0:1, sl])
            ig = _sigmoid(g[:, LRU_BS:] + b_ref[1:2, sl])
            log_a = (-LRU_C) * r * _softplus(-sp_ref[0:1, sl])
            mult = jnp.sqrt(_one_minus_exp(2.0 * log_a))
            mult = jnp.where(first, 1.0, mult)
            _to_segments(a_scr, hb, jnp.exp(log_a), t)
            _to_segments(b_scr, hb, mult * ig * xcb, t)

        @pl.when(at_seq_edge)
        def _():
            carry[...] = jnp.zeros_like(carry)

        @pl.when(d == 0)
        def _():
            _seg_scan_real(a_scr, b_scr, h_scr, carry, t, False)

        @pl.when(d == 1)
        def _():
            _seg_scan_real(a_scr, b_scr, h_scr, carry, t, True)

        for hb in range(LRU_BLOCKS):
            o_ref[:, LRU_BS * hb:LRU_BS * (hb + 1)] = _from_segments(h_scr, hb, t)

    prev, nxt = _halo_specs(t, w, COL["lru_x"], n_t)
    grid_spec = pltpu.PrefetchScalarGridSpec(
        num_scalar_prefetch=2, grid=(2, n_t),
        in_specs=[_tile_spec(t, w, COL["lru_x"], n_t), prev, nxt,
                  _const_spec(conv_w.shape), _const_spec(conv_b.shape),
                  _dir_spec(wcat.shape), _dir_spec(bcat.shape), _dir_spec(sp.shape)],
        out_specs=pl.BlockSpec((None, t, w), lambda d, i, sf, ef: (d, _tile_of(d, i, n_t), 0)),
        scratch_shapes=[pltpu.VMEM((t + 2 * SUBLANES, w), F32)]
        + [pltpu.VMEM((LRU_BLOCKS, _seg_rows(t), LRU_BS), F32)] * 3
        + [pltpu.VMEM((LRU_BLOCKS, SUBLANES, LRU_BS), F32)])
    return pl.pallas_call(
        body, grid_spec=grid_spec, out_shape=jax.ShapeDtypeStruct((2, n, w), F32),
        compiler_params=_cparams(("arbitrary", "arbitrary")), name="lru_sweep")(
            *flags, proj, proj, proj, conv_w, conv_b, wcat, bcat, sp)


def _prep_lru(conv_w, conv_b, w_r, b_r, w_i, b_i, lam):
    wcat = jnp.concatenate([w_r, w_i], axis=-1).astype(BF16)
    bcat = jnp.stack([b_r, b_i], axis=1)
    return conv_w, conv_b.reshape(1, W_BR), wcat, bcat, lam.reshape(2, 1, W_BR)


_NT3 = (((2,), (2,)), ((0,), (0,)))
_NN3 = (((2,), (1,)), ((0,), (0,)))


def _nt(a, b):
    return lax.dot_general(a.astype(BF16), b.astype(BF16), _NT3, preferred_element_type=F32)


def _nn(a, b):
    return lax.dot_general(a.astype(BF16), b.astype(BF16), _NN3, preferred_element_type=F32)


def _rwkv_call(proj, prm, w0, w2, a2, ones_bd, tri, flags, t):
    n = proj.shape[0]
    n_t = n // t
    w = W_BR
    c = RWKV_CHUNK
    n_c = t // c
    nh, hd = RWKV_H, RWKV_HEAD

    def body(sf_ref, ef_ref, r_ref, rp_ref, rn_ref, k_ref, kp_ref, kn_ref, v_ref, vp_ref, vn_ref, lora_ref,
             prm_ref, w0_ref, w2_ref, a2_ref, ones_ref, tri_ref, y_ref, bonus_ref,
             xbuf, at_s, rt_s, bt_s, kt_s, vt_s, yt_s, gam_s, state):
        d = pl.program_id(0)
        ti = _tile_of(d, pl.program_id(1), n_t)
        sf = sf_ref[ti]
        ef = ef_ref[ti]
        keep_p, keep_n = (1 - sf).astype(F32), (1 - ef).astype(F32)
        o = SUBLANES

        def mixed(x_ref, xp_ref, xn_ref, mu):
            _fill_halo(xbuf, x_ref, xp_ref, xn_ref, keep_p, keep_n, t)
            x = xbuf[o:o + t, :]
            return x + mu * (0.5 * (xbuf[o - 1:o - 1 + t, :] + xbuf[o + 1:o + 1 + t, :]) - x)

        r = mixed(r_ref, rp_ref, rn_ref, prm_ref[0:1, :])
        k = mixed(k_ref, kp_ref, kn_ref, prm_ref[1:2, :])
        v = mixed(v_ref, vp_ref, vn_ref, prm_ref[2:3, :])
        lora = lora_ref[...]
        wd = jnp.where(d == 0, lora[:, 0:RWKV_LORA], lora[:, RWKV_LORA:2 * RWKV_LORA])
        ad = lora[:, 2 * RWKV_LORA:3 * RWKV_LORA]
        a_icl = _sigmoid(prm_ref[3:4, :] + _bdot(ad, a2_ref[...]))
        w_log = -_softplus(-(w0_ref[...] + _bdot(jnp.tanh(wd), w2_ref[...]))) - 0.5
        logw = -jnp.exp(w_log)
        kkr = k * prm_ref[4:5, :]
        kk = kkr / jnp.maximum(jnp.sqrt(_split_dot(kkr * kkr, ones_ref[...])), 1e-12)
        k_mod = k * (1.0 + (a_icl - 1.0) * prm_ref[5:6, :])
        bonus_ref[...] = _split_dot(r * k_mod * prm_ref[6:7, :], ones_ref[...]) * v
        b_vec = kk * a_icl

        vt3 = v.T.reshape(nh, hd, t)
        tri_m = tri_ref[...]
        for cc in range(n_c):
            rows = slice(c * cc, c * (cc + 1))
            lw = logw[rows, :]
            hi = lw.astype(BF16)
            rem = lw - hi.astype(F32)
            mid = rem.astype(BF16)
            lo = (rem - mid.astype(F32)).astype(BF16)
            g = (jnp.dot(tri_m, hi, preferred_element_type=F32) + jnp.dot(tri_m, mid, preferred_element_type=F32)
                 + jnp.dot(tri_m, lo, preferred_element_type=F32))
            e_pos = jnp.exp(g)
            e_neg = jnp.exp(-g)
            at_c = -kk[rows, :] * jnp.exp(g - lw)
            rt_c = r[rows, :] * e_pos
            bt_c = b_vec[rows, :] * e_neg
            kt_c = k_mod[rows, :] * e_neg
            g_end = jnp.exp(jnp.where(d == 0, g[c - 1:c, :], g[0:1, :]))
            for h in range(nh):
                ls = slice(hd * h, hd * (h + 1))
                at_s[h, rows, :] = at_c[:, ls]
                rt_s[h, rows, :] = rt_c[:, ls]
                bt_s[h, rows, :] = bt_c[:, ls]
                kt_s[h, rows, :] = kt_c[:, ls]
                gam_s[h, SUBLANES * cc:SUBLANES * (cc + 1), :] = jnp.broadcast_to(g_end[:, ls], (SUBLANES, hd))
            vt_s[cc] = vt3[:, :, rows]

        @pl.when(jnp.where(d == 0, sf, ef) == 1)
        def _():
            state[...] = jnp.zeros_like(state)

        ii = lax.broadcasted_iota(jnp.int32, (c, c), 0)
        jj = lax.broadcasted_iota(jnp.int32, (c, c), 1)
        order = (1 - 2 * d) * (ii - jj)
        strict = (order > 0)[None]
        incl = (order >= 0)[None]
        eye = (ii == jj).astype(F32)[None]

        def chunk(j, carry):
            cc = jnp.where(d == 0, j, n_c - 1 - j)
            rows = pl.ds(pl.multiple_of(cc * c, c), c)
            at, rt, bt, kt = at_s[:, rows, :], rt_s[:, rows, :], bt_s[:, rows, :], kt_s[:, rows, :]
            vt = vt_s[cc]
            gam = gam_s[:, pl.ds(pl.multiple_of(cc * SUBLANES, SUBLANES), 1), :]
            s0 = state[...]
            a_ab = jnp.where(strict, _nt(at, bt), 0.0)
            a_ak = jnp.where(strict, _nt(at, kt), 0.0)
            a_rb = jnp.where(incl, _nt(rt, bt), 0.0)
            a_rk = jnp.where(incl, _nt(rt, kt), 0.0)
            tinv = eye + a_ab
            pw = a_ab
            for _ in range(int(math.log2(c)) - 1):
                pw = _nn(pw, pw)
                tinv = tinv + _nn(pw, tinv)
            p = _nn(tinv, at)
            qt = _nt(_nt(vt, a_ak), tinv)
            ut = _nt(s0, p) + qt
            yt_s[cc] = _nt(s0, rt) + _nt(ut, a_rb) + _nt(vt, a_rk)
            state[...] = (s0 + _nn(ut, bt) + _nn(vt, kt)) * gam
            return carry

        lax.fori_loop(0, n_c, chunk, 0)
        yt = jnp.concatenate([yt_s[cc] for cc in range(n_c)], axis=2)
        y_ref[...] = yt.reshape(w, t).T

    lora_col = COL["rw_wdf"]
    r_halo = _halo_specs(t, w, COL["rw_r"], n_t)
    k_halo = _halo_specs(t, w, COL["rw_k"], n_t)
    v_halo = _halo_specs(t, w, COL["rw_v"], n_t)
    head_major = pltpu.VMEM((nh, t, hd), F32)
    per_chunk = pltpu.VMEM((n_c, nh, hd, c), F32)
    grid_spec = pltpu.PrefetchScalarGridSpec(
        num_scalar_prefetch=2, grid=(2, n_t),
        in_specs=[_tile_spec(t, w, COL["rw_r"], n_t), *r_halo, _tile_spec(t, w, COL["rw_k"], n_t), *k_halo,
                  _tile_spec(t, w, COL["rw_v"], n_t), *v_halo, _tile_spec(t, 4 * RWKV_LORA, lora_col, n_t),
                  _const_spec(prm.shape), _dir_spec(w0.shape), _dir_spec(w2.shape), _const_spec(a2.shape),
                  _const_spec(ones_bd.shape), _dir_spec(tri.shape)],
        out_specs=(pl.BlockSpec((None, t, w), lambda d, i, sf, ef: (d, _tile_of(d, i, n_t), 0)),
                   pl.BlockSpec((None, t, w), lambda d, i, sf, ef: (d, _tile_of(d, i, n_t), 0))),
        scratch_shapes=[pltpu.VMEM((t + 2 * SUBLANES, w), F32), head_major, head_major, head_major, head_major,
                        per_chunk, per_chunk, pltpu.VMEM((nh, n_c * SUBLANES, hd), F32),
                        pltpu.VMEM((nh, hd, hd), F32)])
    return pl.pallas_call(
        body, grid_spec=grid_spec,
        out_shape=(jax.ShapeDtypeStruct((2, n, w), F32), jax.ShapeDtypeStruct((2, n, w), F32)),
        compiler_params=_cparams(("arbitrary", "arbitrary")), name="rwkv_sweep")(
            *flags, proj, proj, proj, proj, proj, proj, proj, proj, proj, proj, prm, w0, w2, a2, ones_bd, tri)


def _prep_rwkv(mu, w0, w2, a0, a2, k_k, k_a, r_k):
    prm = jnp.concatenate([mu, a0[None], k_k[None], k_a[None], r_k.reshape(1, W_BR),
                           jnp.zeros((1, W_BR), F32)], axis=0)
    head = np.arange(W_BR) // RWKV_HEAD
    ones_bd = jnp.asarray(head[:, None] == head[None, :], BF16)
    i = np.arange(RWKV_CHUNK)
    tri = jnp.asarray(np.stack([i[:, None] >= i[None, :], i[:, None] <= i[None, :]]), BF16)
    return prm, w0.reshape(2, 1, W_BR), w2.astype(BF16), a2.astype(BF16), ones_bd, tri


def _head_norm_ref(y, eps, hd):
    yh = y.reshape(y.shape[0], -1, hd)
    mean = jnp.mean(yh, axis=-1, keepdims=True)
    yc = yh - mean
    var = jnp.mean(yc * yc, axis=-1, keepdims=True)
    return (yc * lax.rsqrt(var + eps)).reshape(y.shape)


def _ret_tables(seq_lens):
    half = RET_DK // 2
    pos = jnp.concatenate([jnp.arange(ln, dtype=F32) for ln in seq_lens])
    inv = ROPE_BASE ** (-jnp.arange(half, dtype=F32) / half)
    ang = pos[:, None] * inv[None, :]
    cos, sin = jnp.cos(ang), jnp.sin(ang)
    cos_t = jnp.concatenate([cos, cos], axis=1)
    sin_t = jnp.concatenate([-sin, sin], axis=1)
    c = RET_CHUNK
    log_g = np.log(1.0 - 2.0 ** (-5.0 - np.arange(RET_H, dtype=np.float64)))
    i = np.arange(c, dtype=np.float64)
    dmat = np.exp(log_g[:, None, None] * np.abs(i[:, None] - i[None, :]))
    rows = np.stack([np.exp(log_g[:, None] * (i + 1.0)), np.exp(log_g[:, None] * (c - i)),
                     np.exp(log_g[:, None] * (c - 1.0 - i)), np.exp(log_g[:, None] * i)], axis=1)
    rows = np.broadcast_to(rows[..., None], (RET_H, 4, c, RET_DK))
    return cos_t, sin_t, jnp.asarray(dmat, F32), jnp.asarray(rows, F32)


def _ret_call(proj, cos_t, sin_t, dmat, rowsc, flags, t):
    n = proj.shape[0]
    n_t = n // t
    c = RET_CHUNK
    n_c = t // c
    g_chunk = [float((1.0 - 2.0 ** (-5.0 - h)) ** c) for h in range(RET_H)]
    tn_dims = (((0,), (0,)), ((), ()))
    nt_dims = (((1,), (1,)), ((), ()))

    def body(sf_ref, ef_ref, q_ref, k_ref, v_ref, cos_ref, sin_ref, dm_ref, rs_ref, o_ref, state):
        d = pl.program_id(0)
        ti = _tile_of(d, pl.program_id(1), n_t)
        at_seq_edge = jnp.where(d == 0, sf_ref[ti], ef_ref[ti]) == 1

        @pl.when(at_seq_edge)
        def _():
            state[...] = jnp.zeros_like(state)

        def rot(x_ref, rows, h):
            x = x_ref[rows, RET_DK * h:RET_DK * (h + 1)]
            return x * cos_ref[rows, :] + pltpu.roll(x, RET_DK // 2, 1) * sin_ref[rows, :]

        def chunk(cc, fwd):
            rows = slice(c * cc, c * (cc + 1))
            for h in range(RET_H):
                qh = rot(q_ref, rows, h)
                kh = rot(k_ref, rows, h) * (RET_DK ** -0.5)
                vh = v_ref[rows, RET_DV * h:RET_DV * (h + 1)].astype(BF16)
                s_old = state[h]
                if fwd:
                    sc = lax.dot_general(qh.astype(BF16), kh.astype(BF16), nt_dims,
                                         preferred_element_type=F32) * dm_ref[h]
                    o = (jnp.dot(sc.astype(BF16), vh, preferred_element_type=F32)
                         + _bdot(qh * rs_ref[h, 0], s_old))
                    kd = kh * rs_ref[h, 2]
                else:
                    o = _bdot(qh * rs_ref[h, 1], s_old)
                    kd = kh * rs_ref[h, 3]
                o_ref[rows, RET_DV * h:RET_DV * (h + 1)] = o
                state[h] = g_chunk[h] * s_old + lax.dot_general(kd.astype(BF16), vh, tn_dims,
                                                                 preferred_element_type=F32)

        @pl.when(d == 0)
        def _():
            for cc in range(n_c):
                chunk(cc, True)

        @pl.when(d == 1)
        def _():
            for cc in range(n_c - 1, -1, -1):
                chunk(cc, False)

    hk = RET_H * RET_DK
    tab = pl.BlockSpec((t, RET_DK), lambda d, i, sf, ef: (_tile_of(d, i, n_t), 0))
    grid_spec = pltpu.PrefetchScalarGridSpec(
        num_scalar_prefetch=2, grid=(2, n_t),
        in_specs=[_tile_spec(t, hk, COL["ret_q"], n_t), _tile_spec(t, hk, COL["ret_k"], n_t),
                  _tile_spec(t, W_BR, COL["ret_v"], n_t), tab, tab,
                  _const_spec(dmat.shape), _const_spec(rowsc.shape)],
        out_specs=pl.BlockSpec((None, t, W_BR), lambda d, i, sf, ef: (d, _tile_of(d, i, n_t), 0)),
        scratch_shapes=[pltpu.VMEM((RET_H, RET_DK, RET_DV), F32)])
    return pl.pallas_call(
        body, grid_spec=grid_spec, out_shape=jax.ShapeDtypeStruct((2, n, W_BR), F32),
        compiler_params=_cparams(("arbitrary", "arbitrary")), name="ret_sweep")(
            *flags, proj, proj, proj, cos_t, sin_t, dmat, rowsc)


def _seg_scan_complex(br_ref, bi_ref, sr_ref, si_ref, carry_r, carry_i, base, lam_ref, t, rev):
    s_len = t // SUBLANES
    pitch = _seg_pitch(t)
    nb = br_ref.shape[0]
    blocks = range(nb)
    lr = [jnp.broadcast_to(lam_ref[0:1, LANES * cb:LANES * (cb + 1)], (SUBLANES, LANES)) for cb in blocks]
    li = [jnp.broadcast_to(lam_ref[1:2, LANES * cb:LANES * (cb + 1)], (SUBLANES, LANES)) for cb in blocks]

    def ld(ref, cb, s):
        return ref[cb, pl.ds(s, SUBLANES, stride=pitch), :]

    def step_of(j):
        return s_len - 1 - j if rev else j

    def advance(cb, s, hr, hi):
        nr = lr[cb] * hr - li[cb] * hi + ld(br_ref, cb, s)
        ni = lr[cb] * hi + li[cb] * hr + ld(bi_ref, cb, s)
        return nr, ni

    def pass1(j, h):
        s = step_of(j)
        return tuple(advance(cb, s, *h[cb]) for cb in blocks)

    zero = jnp.zeros((SUBLANES, LANES), F32)
    ends = lax.fori_loop(0, s_len, pass1, ((zero, zero),) * nb, unroll=2)
    starts = []
    for cb in blocks:
        sl = slice(LANES * cb, LANES * (cb + 1))
        pr, pi = lam_ref[2:3, sl], lam_ref[3:4, sl]
        cr, ci = carry_r[base + cb, 0:1, :], carry_i[base + cb, 0:1, :]
        rows_r, rows_i = [None] * SUBLANES, [None] * SUBLANES
        for k in (range(SUBLANES - 1, -1, -1) if rev else range(SUBLANES)):
            rows_r[k], rows_i[k] = cr, ci
            er, ei = ends[cb][0][k:k + 1, :], ends[cb][1][k:k + 1, :]
            cr, ci = pr * cr - pi * ci + er, pr * ci + pi * cr + ei
        carry_r[base + cb, 0:1, :] = cr
        carry_i[base + cb, 0:1, :] = ci
        starts.append((jnp.concatenate(rows_r, axis=0), jnp.concatenate(rows_i, axis=0)))

    def pass2(j, h):
        s = step_of(j)
        out = []
        for cb in blocks:
            nr, ni = advance(cb, s, *h[cb])
            sr_ref[cb, pl.ds(s, SUBLANES, stride=pitch), :] = nr
            si_ref[cb, pl.ds(s, SUBLANES, stride=pitch), :] = ni
            out.append((nr, ni))
        return tuple(out)

    lax.fori_loop(0, s_len, pass2, tuple(starts), unroll=2)


def _s5_call(proj, bblk, cblk, lam4, flags, t):
    n = proj.shape[0]
    n_t = n // t
    w = W_BR
    lb = S5_SW // LANES

    def body(sf_ref, ef_ref, u_ref, b_ref, c_ref, lam_ref, o_ref, br, bi, sr, si, carry_r, carry_i):
        d = pl.program_id(0)
        ti = _tile_of(d, pl.program_id(1), n_t)
        at_seq_edge = jnp.where(d == 0, sf_ref[ti], ef_ref[ti]) == 1

        @pl.when(at_seq_edge)
        def _():
            carry_r[...] = jnp.zeros_like(carry_r)
            carry_i[...] = jnp.zeros_like(carry_i)

        for ob in range(S5_NB):
            bu = _bdot(u_ref[:, LANES * ob:LANES * (ob + 1)], b_ref[ob])
            for cb in range(lb):
                _to_segments(br, cb, bu[:, LANES * cb:LANES * (cb + 1)], t)
                _to_segments(bi, cb, bu[:, S5_SW + LANES * cb:S5_SW + LANES * (cb + 1)], t)

            @pl.when(d == 0)
            def _():
                _seg_scan_complex(br, bi, sr, si, carry_r, carry_i, ob * lb, lam_ref.at[ob], t, False)

            @pl.when(d == 1)
            def _():
                _seg_scan_complex(br, bi, sr, si, carry_r, carry_i, ob * lb, lam_ref.at[ob], t, True)

            st = jnp.concatenate([_from_segments(sr, cb, t) for cb in range(lb)]
                                 + [_from_segments(si, cb, t) for cb in range(lb)], axis=1)
            o_ref[:, LANES * ob:LANES * (ob + 1)] = _bdot(st, c_ref[ob])

    grid_spec = pltpu.PrefetchScalarGridSpec(
        num_scalar_prefetch=2, grid=(2, n_t),
        in_specs=[_tile_spec(t, w, COL["s5_u"], n_t), _const_spec(bblk.shape), _dir_spec(cblk.shape),
                  _dir_spec(lam4.shape)],
        out_specs=pl.BlockSpec((None, t, w), lambda d, i, sf, ef: (d, _tile_of(d, i, n_t), 0)),
        scratch_shapes=[pltpu.VMEM((lb, _seg_rows(t), LANES), F32)] * 4
        + [pltpu.VMEM((S5_NB * lb, SUBLANES, LANES), F32)] * 2)
    return pl.pallas_call(
        body, grid_spec=grid_spec, out_shape=jax.ShapeDtypeStruct((2, n, w), F32),
        compiler_params=_cparams(("arbitrary", "arbitrary")), name="s5_sweep")(
            *flags, proj, bblk, cblk, lam4)


def _prep_s5(lam_re, lam_im, log_step, b_re, b_im, c_re, c_im, t):
    s_len = t // SUBLANES
    assert s_len & (s_len - 1) == 0
    lre = jnp.minimum(lam_re, LAM_RE_MAX)
    step = jnp.exp(log_step)[..., None]
    mag = jnp.exp(lre * step)
    lbr, lbi = mag * jnp.cos(lam_im * step), mag * jnp.sin(lam_im * step)
    den = lre * lre + lam_im * lam_im
    gr = ((lbr - 1.0) * lre + lbi * lam_im) / den
    gi = (lbi * lre - (lbr - 1.0) * lam_im) / den
    pr, pi = lbr, lbi
    for _ in range(int(math.log2(s_len))):
        pr, pi = pr * pr - pi * pi, 2.0 * pr * pi
    lam4 = jnp.stack([x.reshape(2, S5_NB, S5_SW) for x in (lbr, lbi, pr, pi)], axis=2)
    cpr = c_re[None] * gr[:, :, None, :] - c_im[None] * gi[:, :, None, :]
    cpi = c_re[None] * gi[:, :, None, :] + c_im[None] * gr[:, :, None, :]
    eye = jnp.eye(S5_GB, dtype=F32)

    def c_rows(x):
        x = x.reshape(2, S5_NB, S5_GB, S5_P, S5_N)
        return jnp.einsum('dogpn,gh->dognhp', x, eye).reshape(2, S5_NB, S5_SW, S5_GB * S5_P)

    cblk = jnp.concatenate([c_rows(cpr), c_rows(-cpi)], axis=2).astype(BF16)

    def b_cols(x):
        x = x.reshape(S5_NB, S5_GB, S5_N, S5_P)
        return jnp.einsum('ognp,gh->ogphn', x, eye).reshape(S5_NB, S5_GB * S5_P, S5_SW)

    bblk = jnp.concatenate([b_cols(b_re), b_cols(b_im)], axis=2).astype(BF16)
    return bblk, cblk, lam4


def _tile_plan(n, seq_lens):
    shortest = min(seq_lens)
    return dict(
        sweep=min(512, shortest),
        rwkv=min(256, shortest),
        in_m=min(1024, n), in_n=1792,
        fin_m=min(256, n),
        merge_m=min(512, n), merge_n=512,
        out_m=min(256, n),
        norm_m=min(512, n),
    )


def _permute_w_in(w_in):
    orig, start = {}, 0
    for name, nn in _ORIG_SPLITS:
        orig[name] = (start, nn)
        start += nn
    parts = [w_in[:, orig[name][0]:orig[name][0] + orig[name][1]] for name in _NEW_ORDER]
    parts.append(jnp.zeros((w_in.shape[0], _N_PAD), w_in.dtype))
    return jnp.concatenate(parts, axis=1).astype(BF16)


def _trunk(x, seq_lens, norm_g, w_in, lru_conv_w, lru_conv_b, lru_w_r, lru_b_r, lru_w_i, lru_b_i, lru_lambda,
           rwkv_mu, rwkv_w0, rwkv_w2, rwkv_a0, rwkv_a2, rwkv_k_k, rwkv_k_a, rwkv_r_k, rwkv_lnx_g, rwkv_lnx_b,
           ret_gn_g, s5_lam_re, s5_lam_im, s5_log_step, s5_b_re, s5_b_im, s5_c_re, s5_c_im, s5_d, s5_glu_w,
           s5_glu_b, w_branch, w_out, final_g):
    n = x.shape[0]
    tp = _tile_plan(n, seq_lens)
    flags = _seq_flags(seq_lens, tp["sweep"])
    flags_rw = _seq_flags(seq_lens, tp["rwkv"])
    ret_tabs = _ret_tables(seq_lens)
    depth = w_in.shape[0]
    h = _norm_call(x, norm_g[0].reshape(1, D_MODEL), tp["norm_m"])
    for l in range(depth):
        proj = _inproj_call(h, _permute_w_in(w_in[l]), tp["in_m"], tp["in_n"])
        lru_h = _lru_call(proj, *_prep_lru(lru_conv_w[l], lru_conv_b[l], lru_w_r[l], lru_b_r[l], lru_w_i[l],
                                           lru_b_i[l], lru_lambda[l]), flags, tp["sweep"])
        rw_prm = _prep_rwkv(rwkv_mu[l], rwkv_w0[l], rwkv_w2[l], rwkv_a0[l], rwkv_a2[l], rwkv_k_k[l], rwkv_k_a[l],
                            rwkv_r_k[l])
        rw_y, rw_bonus = _rwkv_call(proj, *rw_prm, flags_rw, tp["rwkv"])
        ret_o = _ret_call(proj, *ret_tabs, flags, tp["sweep"])
        s5_y = _s5_call(proj, *_prep_s5(s5_lam_re[l], s5_lam_im[l], s5_log_step[l], s5_b_re[l], s5_b_im[l],
                                        s5_c_re[l], s5_c_im[l], tp["sweep"]), flags, tp["sweep"])
        fprm = jnp.stack([rwkv_lnx_g[l], rwkv_lnx_b[l], ret_gn_g[l], s5_d[l], s5_glu_b[l],
                          jnp.zeros_like(s5_d[l]), jnp.zeros_like(s5_d[l]), jnp.zeros_like(s5_d[l])])
        ycat = _finalize_call(proj, lru_h, rw_y, rw_bonus, ret_o, s5_y, fprm, s5_glu_w[l].astype(BF16),
                              rw_prm[4], tp["fin_m"])
        merged = _merge_call(ycat, proj, w_branch[l].astype(BF16), tp["merge_m"], tp["merge_n"])
        last = l == depth - 1
        g_next = (final_g if last else norm_g[l + 1]).reshape(1, D_MODEL)
        res = _outproj_call(x, merged, w_out[l].astype(BF16), g_next, tp["out_m"], last)
        if last:
            return res
        x, h = res


def kernel(x_prompt, x_sample, norm_g, w_in, lru_conv_w, lru_conv_b, lru_w_r, lru_b_r, lru_w_i, lru_b_i, lru_lambda, rwkv_mu, rwkv_w0, rwkv_w2, rwkv_a0, rwkv_a2, rwkv_k_k, rwkv_k_a, rwkv_r_k, rwkv_lnx_g, rwkv_lnx_b, ret_gn_g, s5_lam_re, s5_lam_im, s5_log_step, s5_b_re, s5_b_im, s5_c_re, s5_c_im, s5_d, s5_glu_w, s5_glu_b, w_branch, w_out, final_g):
    bp, lp, d = x_prompt.shape
    bs, ls, _ = x_sample.shape
    seq_lens = (lp,) * bp + (ls,) * bs
    x = jnp.concatenate([x_prompt.reshape(bp * lp, d), x_sample.reshape(bs * ls, d)], axis=0)
    y = _trunk(x, seq_lens, norm_g, w_in, lru_conv_w, lru_conv_b, lru_w_r, lru_b_r, lru_w_i, lru_b_i, lru_lambda,
               rwkv_mu, rwkv_w0, rwkv_w2, rwkv_a0, rwkv_a2, rwkv_k_k, rwkv_k_a, rwkv_r_k, rwkv_lnx_g, rwkv_lnx_b,
               ret_gn_g, s5_lam_re, s5_lam_im, s5_log_step, s5_b_re, s5_b_im, s5_c_re, s5_c_im, s5_d, s5_glu_w,
               s5_glu_b, w_branch, w_out, final_g)
    return y[:bp * lp].reshape(bp, lp, d), y[bp * lp:].reshape(bs, ls, d)
```

```python
import functools
import math

import numpy as np
import jax
import jax.numpy as jnp
from jax import lax
from jax.experimental import pallas as pl
from jax.experimental.pallas import tpu as pltpu

F32 = jnp.float32
BF16 = jnp.bfloat16

D_MODEL = 2048
DEPTH = 4
W_BR = D_MODEL // 2
N_BRANCH = 4
LRU_BLOCKS = 8
LRU_BS = W_BR // LRU_BLOCKS
LRU_C = 8.0
RWKV_HEAD = 64
RWKV_H = W_BR // RWKV_HEAD
RWKV_LORA = 64
RWKV_LN_EPS = RWKV_HEAD * 1e-5
RWKV_CHUNK = 64
RET_H = 4
RET_DK = W_BR // 2 // RET_H
RET_DV = W_BR // RET_H
RET_CHUNK = 128
RET_GN_EPS = 1e-5
ROPE_BASE = 10000.0
S5_P = 16
S5_G = W_BR // S5_P
S5_N = 64
S5_GB = 8
S5_NB = S5_G // S5_GB
S5_SW = S5_GB * S5_N
LAM_RE_MAX = -1e-4
NORM_EPS = 1e-6

SUBLANES = 8
LANES = 128
VMEM_LIMIT = 56 * 1024 * 1024

_ORIG_SPLITS = (
    ("lru_x", W_BR), ("lru_z", W_BR),
    ("rw_r", W_BR), ("rw_k", W_BR), ("rw_v", W_BR), ("rw_wdf", RWKV_LORA), ("rw_wdb", RWKV_LORA),
    ("rw_ad", RWKV_LORA), ("rw_z", W_BR),
    ("ret_q", RET_H * RET_DK), ("ret_k", RET_H * RET_DK), ("ret_v", W_BR), ("ret_z", W_BR),
    ("s5_u", W_BR), ("s5_z", W_BR),
    ("gates", N_BRANCH * D_MODEL),
)
_NEW_ORDER = ("lru_x", "lru_z", "rw_r", "rw_k", "rw_v", "rw_z", "ret_v", "ret_z", "s5_u", "s5_z", "gates",
              "ret_q", "ret_k", "rw_wdf", "rw_wdb", "rw_ad")


def _column_layout():
    orig, start = {}, 0
    for name, n in _ORIG_SPLITS:
        orig[name] = (start, n)
        start += n
    new, perm, pos = {}, [], 0
    for name in _NEW_ORDER:
        s, n = orig[name]
        new[name] = pos
        perm.extend(range(s, s + n))
        pos += n
    pad = (-pos) % 256
    return new, np.asarray(perm, np.int32), pos, pad


COL, _PERM, _N_IN, _N_PAD = _column_layout()
N_COLS = _N_IN + _N_PAD


def _cparams(sem):
    return pltpu.CompilerParams(dimension_semantics=sem, vmem_limit_bytes=VMEM_LIMIT)


def _sigmoid(x):
    return 1.0 / (1.0 + jnp.exp(-x))


def _softplus(x):
    return jnp.maximum(x, 0.0) + jnp.log1p(jnp.exp(-jnp.abs(x)))


def _one_minus_exp(x):
    series = -x * (1.0 + x * (0.5 + x * (1.0 / 6.0 + x * (1.0 / 24.0))))
    return jnp.where(x > -0.03, series, 1.0 - jnp.exp(x))


def _silu(x):
    return x * _sigmoid(x)


def _bdot(a, b):
    return jnp.dot(a.astype(BF16), b.astype(BF16), preferred_element_type=F32)


def _split_dot(x, w):
    hi = x.astype(BF16)
    lo = (x - hi.astype(F32)).astype(BF16)
    return (jnp.dot(hi, w, preferred_element_type=F32) + jnp.dot(lo, w, preferred_element_type=F32))


def _rmsnorm_rows(x, g):
    return x * lax.rsqrt(jnp.mean(x * x, axis=-1, keepdims=True) + NORM_EPS) * g


def _norm_call(x, g, tm):
    n, d = x.shape

    def body(x_ref, g_ref, o_ref):
        o_ref[...] = _rmsnorm_rows(x_ref[...], g_ref[...]).astype(BF16)

    return pl.pallas_call(
        body, grid=(n // tm,),
        in_specs=[pl.BlockSpec((tm, d), lambda i: (i, 0)), pl.BlockSpec((1, d), lambda i: (0, 0))],
        out_specs=pl.BlockSpec((tm, d), lambda i: (i, 0)),
        out_shape=jax.ShapeDtypeStruct((n, d), BF16),
        compiler_params=_cparams(("parallel",)), name="rmsnorm")(x, g)


def _inproj_call(h, w, tm, tn):
    n, k = h.shape
    nc = w.shape[1]

    def body(h_ref, w_ref, o_ref):
        o_ref[...] = jnp.dot(h_ref[...], w_ref[...], preferred_element_type=F32)

    return pl.pallas_call(
        body, grid=(nc // tn, n // tm),
        in_specs=[pl.BlockSpec((tm, k), lambda j, i: (i, 0)), pl.BlockSpec((k, tn), lambda j, i: (0, j))],
        out_specs=pl.BlockSpec((tm, tn), lambda j, i: (i, j)),
        out_shape=jax.ShapeDtypeStruct((n, nc), F32),
        compiler_params=_cparams(("parallel", "arbitrary")), name="inproj")(h, w)


def _outproj_call(x, merged, w_out, g, tm, last):
    n, d = x.shape

    def body(x_ref, m_ref, w_ref, g_ref, *o_refs):
        y = x_ref[...] + jnp.dot(m_ref[...], w_ref[...], preferred_element_type=F32)
        hn = _rmsnorm_rows(y, g_ref[...])
        if last:
            o_refs[0][...] = hn
        else:
            o_refs[0][...] = y
            o_refs[1][...] = hn.astype(BF16)

    row = pl.BlockSpec((tm, d), lambda i: (i, 0))
    if last:
        out_specs, out_shape = row, jax.ShapeDtypeStruct((n, d), F32)
    else:
        out_specs = (row, row)
        out_shape = (jax.ShapeDtypeStruct((n, d), F32), jax.ShapeDtypeStruct((n, d), BF16))
    return pl.pallas_call(
        body, grid=(n // tm,),
        in_specs=[row, row, pl.BlockSpec((d, d), lambda i: (0, 0)), pl.BlockSpec((1, d), lambda i: (0, 0))],
        out_specs=out_specs, out_shape=out_shape,
        compiler_params=_cparams(("parallel",)), name="outproj")(x, merged, w_out, g)


def _gelu_tanh(x):
    return 0.5 * x * (1.0 + jnp.tanh(math.sqrt(2.0 / math.pi) * (x + 0.044715 * (x * x * x))))


def _finalize_call(proj, lru_h, rw_y, rw_bonus, ret_o, s5_y, fprm, glu_w, ones_bd, tm):
    n = proj.shape[0]
    w = W_BR

    def body(lz_ref, rz_ref, cz_ref, su_ref, sz_ref, lh_ref, ry_ref, rb_ref, co_ref, sy_ref, p_ref, glu_ref,
             ones_ref, o_ref):
        o_ref[0] = ((lh_ref[0] + lh_ref[1]) * _silu(lz_ref[...])).astype(BF16)

        y = ry_ref[0] + ry_ref[1]
        ones = ones_ref[...]
        yc = y - _split_dot(y, ones) * (1.0 / RWKV_HEAD)
        var = _split_dot(yc * yc, ones) * (1.0 / RWKV_HEAD)
        yb = yc * lax.rsqrt(var + RWKV_LN_EPS) * p_ref[0:1, :] + p_ref[1:2, :] + rb_ref[...]
        o_ref[1] = (yb * _silu(rz_ref[...])).astype(BF16)

        for h in range(RET_H):
            sl = slice(RET_DV * h, RET_DV * (h + 1))
            oh = co_ref[0, :, sl] + co_ref[1, :, sl]
            oc = oh - jnp.mean(oh, axis=-1, keepdims=True)
            ov = jnp.mean(oc * oc, axis=-1, keepdims=True)
            yc_h = oc * lax.rsqrt(ov + RET_GN_EPS) * p_ref[2:3, sl]
            o_ref[2, :, sl] = (yc_h * _silu(cz_ref[:, sl])).astype(BF16)

        s = _gelu_tanh(sy_ref[0] + sy_ref[1] + p_ref[3:4, :] * su_ref[...])
        s = s * _sigmoid(_bdot(s, glu_ref[...]) + p_ref[4:5, :])
        o_ref[3] = (s * _silu(sz_ref[...])).astype(BF16)

    def col(name):
        cb = COL[name] // w
        return pl.BlockSpec((tm, w), lambda i: (i, cb))

    both = pl.BlockSpec((2, tm, w), lambda i: (0, i, 0))
    first = pl.BlockSpec((None, tm, w), lambda i: (0, i, 0))

    def const(shape):
        return pl.BlockSpec(tuple(shape), lambda i: (0,) * len(shape))

    return pl.pallas_call(
        body, grid=(n // tm,),
        in_specs=[col("lru_z"), col("rw_z"), col("ret_z"), col("s5_u"), col("s5_z"), both, both, first, both, both,
                  const(fprm.shape), const(glu_w.shape), const(ones_bd.shape)],
        out_specs=pl.BlockSpec((N_BRANCH, tm, w), lambda i: (0, i, 0)),
        out_shape=jax.ShapeDtypeStruct((N_BRANCH, n, w), BF16),
        compiler_params=_cparams(("parallel",)), name="finalize")(
            proj, proj, proj, proj, proj, lru_h, rw_y, rw_bonus, ret_o, s5_y, fprm, glu_w, ones_bd)


def _merge_call(ycat, proj, w_branch, tm, tn):
    n = proj.shape[0]
    gate0 = COL["gates"] // tn
    per_branch = D_MODEL // tn

    def body(y_ref, g0, g1, g2, g3, w_ref, o_ref):
        acc = None
        for b, g_ref in enumerate((g0, g1, g2, g3)):
            term = _sigmoid(g_ref[...]) * jnp.dot(y_ref[b], w_ref[b], preferred_element_type=F32)
            acc = term if acc is None else acc + term
        o_ref[...] = acc.astype(BF16)

    gates = [pl.BlockSpec((tm, tn), lambda j, i, b=b: (i, gate0 + b * per_branch + j)) for b in range(N_BRANCH)]
    return pl.pallas_call(
        body, grid=(D_MODEL // tn, n // tm),
        in_specs=[pl.BlockSpec((N_BRANCH, tm, W_BR), lambda j, i: (0, i, 0)), *gates,
                  pl.BlockSpec((N_BRANCH, W_BR, tn), lambda j, i: (0, 0, j))],
        out_specs=pl.BlockSpec((tm, tn), lambda j, i: (i, j)),
        out_shape=jax.ShapeDtypeStruct((n, D_MODEL), BF16),
        compiler_params=_cparams(("parallel", "arbitrary")), name="merge")(
            ycat, proj, proj, proj, proj, w_branch)


def _seq_flags(seq_lens, t):
    sf, ef = [], []
    for ln in seq_lens:
        assert ln % t == 0
        k = ln // t
        sf += [1] + [0] * (k - 1)
        ef += [0] * (k - 1) + [1]
    return jnp.asarray(sf, jnp.int32), jnp.asarray(ef, jnp.int32)


def _tile_of(d, i, n_t):
    return i + d * (n_t - 1 - 2 * i)


def _tile_spec(t, w, col, n_t):
    cb = col // w
    return pl.BlockSpec((t, w), lambda d, i, sf, ef: (_tile_of(d, i, n_t), cb))


def _halo_specs(t, w, col, n_t):
    cb = col // w
    hb = t // SUBLANES
    last = n_t * hb - 1
    prev = pl.BlockSpec((SUBLANES, w), lambda d, i, sf, ef: (jnp.maximum(_tile_of(d, i, n_t) * hb - 1, 0), cb))
    nxt = pl.BlockSpec((SUBLANES, w), lambda d, i, sf, ef: (jnp.minimum((_tile_of(d, i, n_t) + 1) * hb, last), cb))
    return prev, nxt


def _dir_spec(shape):
    nd = len(shape)
    return pl.BlockSpec((None,) + tuple(shape[1:]), lambda d, i, sf, ef: (d,) + (0,) * (nd - 1))


def _const_spec(shape):
    nd = len(shape)
    return pl.BlockSpec(tuple(shape), lambda d, i, sf, ef: (0,) * nd)


def _fill_halo(buf, x_ref, xp_ref, xn_ref, keep_p, keep_n, t):
    buf[0:SUBLANES, :] = xp_ref[...] * keep_p
    buf[SUBLANES:t + SUBLANES, :] = x_ref[...]
    buf[t + SUBLANES:t + 2 * SUBLANES, :] = xn_ref[...] * keep_n


def _seg_pitch(t):
    return t // SUBLANES + 4


def _seg_rows(t):
    return SUBLANES * _seg_pitch(t)


def _to_segments(ref, blk, val, t):
    s_len, pitch = t // SUBLANES, _seg_pitch(t)
    for k in range(SUBLANES):
        ref[blk, pitch * k:pitch * k + s_len, :] = val[s_len * k:s_len * (k + 1), :]


def _from_segments(ref, blk, t):
    s_len, pitch = t // SUBLANES, _seg_pitch(t)
    return jnp.concatenate([ref[blk, pitch * k:pitch * k + s_len, :] for k in range(SUBLANES)], axis=0)


def _seg_scan_real(a_ref, b_ref, o_ref, carry_ref, t, rev):
    s_len = t // SUBLANES
    pitch = _seg_pitch(t)
    nb = a_ref.shape[0]
    blocks = range(nb)

    def ld(ref, cb, s):
        return ref[cb, pl.ds(s, SUBLANES, stride=pitch), :]

    def step_of(j):
        return s_len - 1 - j if rev else j

    def pass1(j, hp):
        hs, ps = hp
        s = step_of(j)
        a = [ld(a_ref, cb, s) for cb in blocks]
        return (tuple(a[cb] * hs[cb] + ld(b_ref, cb, s) for cb in blocks),
                tuple(a[cb] * ps[cb] for cb in blocks))

    zero = jnp.zeros((SUBLANES, LANES), F32)
    es, ps = lax.fori_loop(0, s_len, pass1, ((zero,) * nb, (zero + 1.0,) * nb), unroll=2)
    starts = []
    for cb in blocks:
        c = carry_ref[cb, 0:1, :]
        rows = [None] * SUBLANES
        for k in (range(SUBLANES - 1, -1, -1) if rev else range(SUBLANES)):
            rows[k] = c
            c = ps[cb][k:k + 1, :] * c + es[cb][k:k + 1, :]
        carry_ref[cb, 0:1, :] = c
        starts.append(jnp.concatenate(rows, axis=0))

    def pass2(j, hs):
        s = step_of(j)
        out = []
        for cb in blocks:
            h = ld(a_ref, cb, s) * hs[cb] + ld(b_ref, cb, s)
            o_ref[cb, pl.ds(s, SUBLANES, stride=pitch), :] = h
            out.append(h)
        return tuple(out)

    lax.fori_loop(0, s_len, pass2, tuple(starts), unroll=2)


def _lru_call(proj, conv_w, conv_b, wcat, bcat, sp, flags, t):
    n = proj.shape[0]
    n_t = n // t
    w = W_BR

    def body(sf_ref, ef_ref, x_ref, xp_ref, xn_ref, cw_ref, cb_ref, w_ref, b_ref, sp_ref, o_ref,
             xbuf, a_scr, b_scr, h_scr, carry):
        d = pl.program_id(0)
        ti = _tile_of(d, pl.program_id(1), n_t)
        sf = sf_ref[ti]
        ef = ef_ref[ti]
        _fill_halo(xbuf, x_ref, xp_ref, xn_ref, (1 - sf).astype(F32), (1 - ef).astype(F32), t)
        o = SUBLANES
        xc = (cw_ref[0:1, :] * xbuf[o - 2:o - 2 + t, :] + cw_ref[1:2, :] * xbuf[o - 1:o - 1 + t, :]
              + cw_ref[2:3, :] * xbuf[o:o + t, :] + cw_ref[3:4, :] * xbuf[o + 1:o + 1 + t, :] + cb_ref[...])
        row = lax.broadcasted_iota(jnp.int32, (t, LRU_BS), 0)
        first_row = jnp.where(d == 0, 0, t - 1)
        at_seq_edge = jnp.where(d == 0, sf, ef) == 1
        first = jnp.logical_and(row == first_row, at_seq_edge)
        for hb in range(LRU_BLOCKS):
            sl = slice(LRU_BS * hb, LRU_BS * (hb + 1))
            xcb = xc[:, sl]
            g = _bdot(xcb, w_ref[hb])
            r = _sigmoid(g[:, :LRU_BS] + b_ref[0:1, sl])
            ig = _sigmoid(g[:, LRU_BS:] + b_ref[1:2, sl])
            log_a = (-LRU_C) * r * _softplus(-sp_ref[0:1, sl])
            mult = jnp.sqrt(_one_minus_exp(2.0 * log_a))
            mult = jnp.where(first, 1.0, mult)
            _to_segments(a_scr, hb, jnp.exp(log_a), t)
            _to_segments(b_scr, hb, mult * ig * xcb, t)

        @pl.when(at_seq_edge)
        def _():
            carry[...] = jnp.zeros_like(carry)

        @pl.when(d == 0)
        def _():
            _seg_scan_real(a_scr, b_scr, h_scr, carry, t, False)

        @pl.when(d == 1)
        def _():
            _seg_scan_real(a_scr, b_scr, h_scr, carry, t, True)

        for hb in range(LRU_BLOCKS):
            o_ref[:, LRU_BS * hb:LRU_BS * (hb + 1)] = _from_segments(h_scr, hb, t)

    prev, nxt = _halo_specs(t, w, COL["lru_x"], n_t)
    grid_spec = pltpu.PrefetchScalarGridSpec(
        num_scalar_prefetch=2, grid=(2, n_t),
        in_specs=[_tile_spec(t, w, COL["lru_x"], n_t), prev, nxt,
                  _const_spec(conv_w.shape), _const_spec(conv_b.shape),
                  _dir_spec(wcat.shape), _dir_spec(bcat.shape), _dir_spec(sp.shape)],
        out_specs=pl.BlockSpec((None, t, w), lambda d, i, sf, ef: (d, _tile_of(d, i, n_t), 0)),
        scratch_shapes=[pltpu.VMEM((t + 2 * SUBLANES, w), F32)]
        + [pltpu.VMEM((LRU_BLOCKS, _seg_rows(t), LRU_BS), F32)] * 3
        + [pltpu.VMEM((LRU_BLOCKS, SUBLANES, LRU_BS), F32)])
    return pl.pallas_call(
        body, grid_spec=grid_spec, out_shape=jax.ShapeDtypeStruct((2, n, w), F32),
        compiler_params=_cparams(("arbitrary", "arbitrary")), name="lru_sweep")(
            *flags, proj, proj, proj, conv_w, conv_b, wcat, bcat, sp)


def _prep_lru(conv_w, conv_b, w_r, b_r, w_i, b_i, lam):
    wcat = jnp.concatenate([w_r, w_i], axis=-1).astype(BF16)
    bcat = jnp.stack([b_r, b_i], axis=1)
    return conv_w, conv_b.reshape(1, W_BR), wcat, bcat, lam.reshape(2, 1, W_BR)


_NT3 = (((2,), (2,)), ((0,), (0,)))
_NN3 = (((2,), (1,)), ((0,), (0,)))


def _nt(a, b):
    return lax.dot_general(a.astype(BF16), b.astype(BF16), _NT3, preferred_element_type=F32)


def _nn(a, b):
    return lax.dot_general(a.astype(BF16), b.astype(BF16), _NN3, preferred_element_type=F32)


def _rwkv_call(proj, prm, w0, w2, a2, ones_bd, tri, flags, t):
    n = proj.shape[0]
    n_t = n // t
    w = W_BR
    c = RWKV_CHUNK
    n_c = t // c
    nh, hd = RWKV_H, RWKV_HEAD

    def body(sf_ref, ef_ref, r_ref, rp_ref, rn_ref, k_ref, kp_ref, kn_ref, v_ref, vp_ref, vn_ref, lora_ref,
             prm_ref, w0_ref, w2_ref, a2_ref, ones_ref, tri_ref, y_ref, bonus_ref,
             xbuf, art_s, bk_s, vt_s, yt_s, gam_s, state):
        d = pl.program_id(0)
        ti = _tile_of(d, pl.program_id(1), n_t)
        sf = sf_ref[ti]
        ef = ef_ref[ti]
        keep_p, keep_n = (1 - sf).astype(F32), (1 - ef).astype(F32)
        o = SUBLANES

        def mixed(x_ref, xp_ref, xn_ref, mu):
            _fill_halo(xbuf, x_ref, xp_ref, xn_ref, keep_p, keep_n, t)
            x = xbuf[o:o + t, :]
            return x + mu * (0.5 * (xbuf[o - 1:o - 1 + t, :] + xbuf[o + 1:o + 1 + t, :]) - x)

        r = mixed(r_ref, rp_ref, rn_ref, prm_ref[0:1, :])
        k = mixed(k_ref, kp_ref, kn_ref, prm_ref[1:2, :])
        v = mixed(v_ref, vp_ref, vn_ref, prm_ref[2:3, :])
        lora = lora_ref[...]
        wd = jnp.where(d == 0, lora[:, 0:RWKV_LORA], lora[:, RWKV_LORA:2 * RWKV_LORA])
        ad = lora[:, 2 * RWKV_LORA:3 * RWKV_LORA]
        a_icl = _sigmoid(prm_ref[3:4, :] + _bdot(ad, a2_ref[...]))
        w_log = -_softplus(-(w0_ref[...] + _bdot(jnp.tanh(wd), w2_ref[...]))) - 0.5
        logw = -jnp.exp(w_log)
        kkr = k * prm_ref[4:5, :]
        kk = kkr / jnp.maximum(jnp.sqrt(_split_dot(kkr * kkr, ones_ref[...])), 1e-12)
        k_mod = k * (1.0 + (a_icl - 1.0) * prm_ref[5:6, :])
        bonus_ref[...] = _split_dot(r * k_mod * prm_ref[6:7, :], ones_ref[...]) * v
        b_vec = kk * a_icl

        vt3 = v.T.reshape(nh, hd, t)
        tri_m = tri_ref[...]
        for cc in range(n_c):
            rows = slice(c * cc, c * (cc + 1))
            lw = logw[rows, :]
            hi = lw.astype(BF16)
            rem = lw - hi.astype(F32)
            mid = rem.astype(BF16)
            lo = (rem - mid.astype(F32)).astype(BF16)
            g = (jnp.dot(tri_m, hi, preferred_element_type=F32) + jnp.dot(tri_m, mid, preferred_element_type=F32)
                 + jnp.dot(tri_m, lo, preferred_element_type=F32))
            e_pos = jnp.exp(g)
            e_neg = jnp.exp(-g)
            at_c = -kk[rows, :] * jnp.exp(g - lw)
            rt_c = r[rows, :] * e_pos
            bt_c = b_vec[rows, :] * e_neg
            kt_c = k_mod[rows, :] * e_neg
            g_end = jnp.exp(jnp.where(d == 0, g[c - 1:c, :], g[0:1, :]))
            for h in range(nh):
                ls = slice(hd * h, hd * (h + 1))
                art_s[cc, h, 0:c, :] = at_c[:, ls]
                art_s[cc, h, c:2 * c, :] = rt_c[:, ls]
                bk_s[cc, h, 0:c, :] = bt_c[:, ls]
                bk_s[cc, h, c:2 * c, :] = kt_c[:, ls]
                gam_s[h, SUBLANES * cc:SUBLANES * (cc + 1), :] = jnp.broadcast_to(g_end[:, ls], (SUBLANES, hd))
            vt_c = vt3[:, :, rows]
            vt_s[cc] = jnp.concatenate([jnp.zeros_like(vt_c), vt_c], axis=2)

        @pl.when(jnp.where(d == 0, sf, ef) == 1)
        def _():
            state[...] = jnp.zeros_like(state)

        ii = lax.broadcasted_iota(jnp.int32, (c, 2 * c), 0)
        jj = lax.broadcasted_iota(jnp.int32, (c, 2 * c), 1)
        jj = jnp.where(jj >= c, jj - c, jj)
        order = (1 - 2 * d) * (ii - jj)
        strict = (order > 0)[None]
        incl = (order >= 0)[None]
        eye = (lax.broadcasted_iota(jnp.int32, (c, c), 0)
               == lax.broadcasted_iota(jnp.int32, (c, c), 1)).astype(F32)[None]

        def chunk(j, carry):
            cc = jnp.where(d == 0, j, n_c - 1 - j)
            art = art_s[cc]
            bk = bk_s[cc]
            at, rt = art[:, :c], art[:, c:]
            vt_hi = vt_s[cc]
            gam = gam_s[:, pl.ds(pl.multiple_of(cc * SUBLANES, SUBLANES), 1), :]
            s0 = state[...]
            a_all = _nt(art, bk)
            top = jnp.where(strict, a_all[:, :c], 0.0)
            bot = jnp.where(incl, a_all[:, c:], 0.0)
            a_ab = top[:, :, :c]
            tinv = eye + a_ab
            pw = a_ab
            for _ in range(int(math.log2(c)) - 1):
                pw = _nn(pw, pw)
                tinv = tinv + _nn(pw, tinv)
            zt = _nt(s0, at) + _nt(vt_hi, top)
            ut = _nt(zt, tinv)
            ut_vt = jnp.concatenate([ut, jnp.zeros_like(ut)], axis=2) + vt_hi
            yt_s[cc] = _nt(s0, rt) + _nt(ut_vt, bot)
            state[...] = (s0 + _nn(ut_vt, bk)) * gam
            return carry

        lax.fori_loop(0, n_c, chunk, 0)
        yt = jnp.concatenate([yt_s[cc] for cc in range(n_c)], axis=2)
        y_ref[...] = yt.reshape(w, t).T

    lora_col = COL["rw_wdf"]
    r_halo = _halo_specs(t, w, COL["rw_r"], n_t)
    k_halo = _halo_specs(t, w, COL["rw_k"], n_t)
    v_halo = _halo_specs(t, w, COL["rw_v"], n_t)
    stacked = pltpu.VMEM((n_c, nh, 2 * c, hd), F32)
    grid_spec = pltpu.PrefetchScalarGridSpec(
        num_scalar_prefetch=2, grid=(2, n_t),
        in_specs=[_tile_spec(t, w, COL["rw_r"], n_t), *r_halo, _tile_spec(t, w, COL["rw_k"], n_t), *k_halo,
                  _tile_spec(t, w, COL["rw_v"], n_t), *v_halo, _tile_spec(t, 4 * RWKV_LORA, lora_col, n_t),
                  _const_spec(prm.shape), _dir_spec(w0.shape), _dir_spec(w2.shape), _const_spec(a2.shape),
                  _const_spec(ones_bd.shape), _dir_spec(tri.shape)],
        out_specs=(pl.BlockSpec((None, t, w), lambda d, i, sf, ef: (d, _tile_of(d, i, n_t), 0)),
                   pl.BlockSpec((None, t, w), lambda d, i, sf, ef: (d, _tile_of(d, i, n_t), 0))),
        scratch_shapes=[pltpu.VMEM((t + 2 * SUBLANES, w), F32), stacked, stacked,
                        pltpu.VMEM((n_c, nh, hd, 2 * c), F32), pltpu.VMEM((n_c, nh, hd, c), F32),
                        pltpu.VMEM((nh, n_c * SUBLANES, hd), F32),
                        pltpu.VMEM((nh, hd, hd), F32)])
    return pl.pallas_call(
        body, grid_spec=grid_spec,
        out_shape=(jax.ShapeDtypeStruct((2, n, w), F32), jax.ShapeDtypeStruct((2, n, w), F32)),
        compiler_params=_cparams(("arbitrary", "arbitrary")), name="rwkv_sweep")(
            *flags, proj, proj, proj, proj, proj, proj, proj, proj, proj, proj, prm, w0, w2, a2, ones_bd, tri)


def _prep_rwkv(mu, w0, w2, a0, a2, k_k, k_a, r_k):
    prm = jnp.concatenate([mu, a0[None], k_k[None], k_a[None], r_k.reshape(1, W_BR),
                           jnp.zeros((1, W_BR), F32)], axis=0)
    head = np.arange(W_BR) // RWKV_HEAD
    ones_bd = jnp.asarray(head[:, None] == head[None, :], BF16)
    i = np.arange(RWKV_CHUNK)
    tri = jnp.asarray(np.stack([i[:, None] >= i[None, :], i[:, None] <= i[None, :]]), BF16)
    return prm, w0.reshape(2, 1, W_BR), w2.astype(BF16), a2.astype(BF16), ones_bd, tri


def _head_norm_ref(y, eps, hd):
    yh = y.reshape(y.shape[0], -1, hd)
    mean = jnp.mean(yh, axis=-1, keepdims=True)
    yc = yh - mean
    var = jnp.mean(yc * yc, axis=-1, keepdims=True)
    return (yc * lax.rsqrt(var + eps)).reshape(y.shape)


def _ret_tables(seq_lens):
    half = RET_DK // 2
    pos = jnp.concatenate([jnp.arange(ln, dtype=F32) for ln in seq_lens])
    inv = ROPE_BASE ** (-jnp.arange(half, dtype=F32) / half)
    ang = pos[:, None] * inv[None, :]
    cos, sin = jnp.cos(ang), jnp.sin(ang)
    cos_t = jnp.concatenate([cos, cos], axis=1)
    sin_t = jnp.concatenate([-sin, sin], axis=1)
    c = RET_CHUNK
    log_g = np.log(1.0 - 2.0 ** (-5.0 - np.arange(RET_H, dtype=np.float64)))
    i = np.arange(c, dtype=np.float64)
    dmat = np.exp(log_g[:, None, None] * np.abs(i[:, None] - i[None, :]))
    rows = np.stack([np.exp(log_g[:, None] * (i + 1.0)), np.exp(log_g[:, None] * (c - i)),
                     np.exp(log_g[:, None] * (c - 1.0 - i)), np.exp(log_g[:, None] * i)], axis=1)
    rows = np.broadcast_to(rows[..., None], (RET_H, 4, c, RET_DK))
    return cos_t, sin_t, jnp.asarray(dmat, F32), jnp.asarray(rows, F32)


def _ret_call(proj, cos_t, sin_t, dmat, rowsc, flags, t):
    n = proj.shape[0]
    n_t = n // t
    c = RET_CHUNK
    n_c = t // c
    g_chunk = [float((1.0 - 2.0 ** (-5.0 - h)) ** c) for h in range(RET_H)]
    tn_dims = (((0,), (0,)), ((), ()))
    nt_dims = (((1,), (1,)), ((), ()))

    def body(sf_ref, ef_ref, q_ref, k_ref, v_ref, cos_ref, sin_ref, dm_ref, rs_ref, o_ref, state):
        d = pl.program_id(0)
        ti = _tile_of(d, pl.program_id(1), n_t)
        at_seq_edge = jnp.where(d == 0, sf_ref[ti], ef_ref[ti]) == 1

        @pl.when(at_seq_edge)
        def _():
            state[...] = jnp.zeros_like(state)

        def rot(x_ref, rows, h):
            x = x_ref[rows, RET_DK * h:RET_DK * (h + 1)]
            return x * cos_ref[rows, :] + pltpu.roll(x, RET_DK // 2, 1) * sin_ref[rows, :]

        def chunk(cc, fwd):
            rows = slice(c * cc, c * (cc + 1))
            for h in range(RET_H):
                qh = rot(q_ref, rows, h)
                kh = rot(k_ref, rows, h) * (RET_DK ** -0.5)
                vh = v_ref[rows, RET_DV * h:RET_DV * (h + 1)].astype(BF16)
                s_old = state[h]
                if fwd:
                    sc = lax.dot_general(qh.astype(BF16), kh.astype(BF16), nt_dims,
                                         preferred_element_type=F32) * dm_ref[h]
                    o = (jnp.dot(sc.astype(BF16), vh, preferred_element_type=F32)
                         + _bdot(qh * rs_ref[h, 0], s_old))
                    kd = kh * rs_ref[h, 2]
                else:
                    o = _bdot(qh * rs_ref[h, 1], s_old)
                    kd = kh * rs_ref[h, 3]
                o_ref[rows, RET_DV * h:RET_DV * (h + 1)] = o
                state[h] = g_chunk[h] * s_old + lax.dot_general(kd.astype(BF16), vh, tn_dims,
                                                                 preferred_element_type=F32)

        @pl.when(d == 0)
        def _():
            for cc in range(n_c):
                chunk(cc, True)

        @pl.when(d == 1)
        def _():
            for cc in range(n_c - 1, -1, -1):
                chunk(cc, False)

    hk = RET_H * RET_DK
    tab = pl.BlockSpec((t, RET_DK), lambda d, i, sf, ef: (_tile_of(d, i, n_t), 0))
    grid_spec = pltpu.PrefetchScalarGridSpec(
        num_scalar_prefetch=2, grid=(2, n_t),
        in_specs=[_tile_spec(t, hk, COL["ret_q"], n_t), _tile_spec(t, hk, COL["ret_k"], n_t),
                  _tile_spec(t, W_BR, COL["ret_v"], n_t), tab, tab,
                  _const_spec(dmat.shape), _const_spec(rowsc.shape)],
        out_specs=pl.BlockSpec((None, t, W_BR), lambda d, i, sf, ef: (d, _tile_of(d, i, n_t), 0)),
        scratch_shapes=[pltpu.VMEM((RET_H, RET_DK, RET_DV), F32)])
    return pl.pallas_call(
        body, grid_spec=grid_spec, out_shape=jax.ShapeDtypeStruct((2, n, W_BR), F32),
        compiler_params=_cparams(("arbitrary", "arbitrary")), name="ret_sweep")(
            *flags, proj, proj, proj, cos_t, sin_t, dmat, rowsc)


def _seg_scan_complex(br_ref, bi_ref, sr_ref, si_ref, carry_r, carry_i, base, lam_ref, t, rev):
    s_len = t // SUBLANES
    pitch = _seg_pitch(t)
    nb = br_ref.shape[0]
    blocks = range(nb)
    lr = [jnp.broadcast_to(lam_ref[0:1, LANES * cb:LANES * (cb + 1)], (SUBLANES, LANES)) for cb in blocks]
    li = [jnp.broadcast_to(lam_ref[1:2, LANES * cb:LANES * (cb + 1)], (SUBLANES, LANES)) for cb in blocks]

    def ld(ref, cb, s):
        return ref[cb, pl.ds(s, SUBLANES, stride=pitch), :]

    def step_of(j):
        return s_len - 1 - j if rev else j

    def advance(cb, s, hr, hi):
        nr = lr[cb] * hr - li[cb] * hi + ld(br_ref, cb, s)
        ni = lr[cb] * hi + li[cb] * hr + ld(bi_ref, cb, s)
        return nr, ni

    def pass1(j, h):
        s = step_of(j)
        return tuple(advance(cb, s, *h[cb]) for cb in blocks)

    zero = jnp.zeros((SUBLANES, LANES), F32)
    ends = lax.fori_loop(0, s_len, pass1, ((zero, zero),) * nb, unroll=2)
    starts = []
    for cb in blocks:
        sl = slice(LANES * cb, LANES * (cb + 1))
        pr, pi = lam_ref[2:3, sl], lam_ref[3:4, sl]
        cr, ci = carry_r[base + cb, 0:1, :], carry_i[base + cb, 0:1, :]
        rows_r, rows_i = [None] * SUBLANES, [None] * SUBLANES
        for k in (range(SUBLANES - 1, -1, -1) if rev else range(SUBLANES)):
            rows_r[k], rows_i[k] = cr, ci
            er, ei = ends[cb][0][k:k + 1, :], ends[cb][1][k:k + 1, :]
            cr, ci = pr * cr - pi * ci + er, pr * ci + pi * cr + ei
        carry_r[base + cb, 0:1, :] = cr
        carry_i[base + cb, 0:1, :] = ci
        starts.append((jnp.concatenate(rows_r, axis=0), jnp.concatenate(rows_i, axis=0)))

    def pass2(j, h):
        s = step_of(j)
        out = []
        for cb in blocks:
            nr, ni = advance(cb, s, *h[cb])
            sr_ref[cb, pl.ds(s, SUBLANES, stride=pitch), :] = nr
            si_ref[cb, pl.ds(s, SUBLANES, stride=pitch), :] = ni
            out.append((nr, ni))
        return tuple(out)

    lax.fori_loop(0, s_len, pass2, tuple(starts), unroll=2)


def _s5_call(proj, bblk, cblk, lam4, flags, t):
    n = proj.shape[0]
    n_t = n // t
    w = W_BR
    lb = S5_SW // LANES

    def body(sf_ref, ef_ref, u_ref, b_ref, c_ref, lam_ref, o_ref, br, bi, sr, si, carry_r, carry_i):
        d = pl.program_id(0)
        ti = _tile_of(d, pl.program_id(1), n_t)
        at_seq_edge = jnp.where(d == 0, sf_ref[ti], ef_ref[ti]) == 1

        @pl.when(at_seq_edge)
        def _():
            carry_r[...] = jnp.zeros_like(carry_r)
            carry_i[...] = jnp.zeros_like(carry_i)

        for ob in range(S5_NB):
            bu = _bdot(u_ref[:, LANES * ob:LANES * (ob + 1)], b_ref[ob])
            for cb in range(lb):
                _to_segments(br, cb, bu[:, LANES * cb:LANES * (cb + 1)], t)
                _to_segments(bi, cb, bu[:, S5_SW + LANES * cb:S5_SW + LANES * (cb + 1)], t)

            @pl.when(d == 0)
            def _():
                _seg_scan_complex(br, bi, sr, si, carry_r, carry_i, ob * lb, lam_ref.at[ob], t, False)

            @pl.when(d == 1)
            def _():
                _seg_scan_complex(br, bi, sr, si, carry_r, carry_i, ob * lb, lam_ref.at[ob], t, True)

            st = jnp.concatenate([_from_segments(sr, cb, t) for cb in range(lb)]
                                 + [_from_segments(si, cb, t) for cb in range(lb)], axis=1)
            o_ref[:, LANES * ob:LANES * (ob + 1)] = _bdot(st, c_ref[ob])

    grid_spec = pltpu.PrefetchScalarGridSpec(
        num_scalar_prefetch=2, grid=(2, n_t),
        in_specs=[_tile_spec(t, w, COL["s5_u"], n_t), _const_spec(bblk.shape), _dir_spec(cblk.shape),
                  _dir_spec(lam4.shape)],
        out_specs=pl.BlockSpec((None, t, w), lambda d, i, sf, ef: (d, _tile_of(d, i, n_t), 0)),
        scratch_shapes=[pltpu.VMEM((lb, _seg_rows(t), LANES), F32)] * 4
        + [pltpu.VMEM((S5_NB * lb, SUBLANES, LANES), F32)] * 2)
    return pl.pallas_call(
        body, grid_spec=grid_spec, out_shape=jax.ShapeDtypeStruct((2, n, w), F32),
        compiler_params=_cparams(("arbitrary", "arbitrary")), name="s5_sweep")(
            *flags, proj, bblk, cblk, lam4)


def _prep_s5(lam_re, lam_im, log_step, b_re, b_im, c_re, c_im, t):
    s_len = t // SUBLANES
    assert s_len & (s_len - 1) == 0
    lre = jnp.minimum(lam_re, LAM_RE_MAX)
    step = jnp.exp(log_step)[..., None]
    mag = jnp.exp(lre * step)
    lbr, lbi = mag * jnp.cos(lam_im * step), mag * jnp.sin(lam_im * step)
    den = lre * lre + lam_im * lam_im
    gr = ((lbr - 1.0) * lre + lbi * lam_im) / den
    gi = (lbi * lre - (lbr - 1.0) * lam_im) / den
    pr, pi = lbr, lbi
    for _ in range(int(math.log2(s_len))):
        pr, pi = pr * pr - pi * pi, 2.0 * pr * pi
    lam4 = jnp.stack([x.reshape(2, S5_NB, S5_SW) for x in (lbr, lbi, pr, pi)], axis=2)
    cpr = c_re[None] * gr[:, :, None, :] - c_im[None] * gi[:, :, None, :]
    cpi = c_re[None] * gi[:, :, None, :] + c_im[None] * gr[:, :, None, :]
    eye = jnp.eye(S5_GB, dtype=F32)

    def c_rows(x):
        x = x.reshape(2, S5_NB, S5_GB, S5_P, S5_N)
        return jnp.einsum('dogpn,gh->dognhp', x, eye).reshape(2, S5_NB, S5_SW, S5_GB * S5_P)

    cblk = jnp.concatenate([c_rows(cpr), c_rows(-cpi)], axis=2).astype(BF16)

    def b_cols(x):
        x = x.reshape(S5_NB, S5_GB, S5_N, S5_P)
        return jnp.einsum('ognp,gh->ogphn', x, eye).reshape(S5_NB, S5_GB * S5_P, S5_SW)

    bblk = jnp.concatenate([b_cols(b_re), b_cols(b_im)], axis=2).astype(BF16)
    return bblk, cblk, lam4


def _tile_plan(n, seq_lens):
    shortest = min(seq_lens)
    return dict(
        sweep=min(512, shortest),
        rwkv=min(256, shortest),
        in_m=min(1024, n), in_n=1792,
        fin_m=min(256, n),
        merge_m=min(512, n), merge_n=1024,
        out_m=min(256, n),
        norm_m=min(512, n),
    )


def _permute_w_in(w_in):
    orig, start = {}, 0
    for name, nn in _ORIG_SPLITS:
        orig[name] = (start, nn)
        start += nn
    parts = [w_in[:, orig[name][0]:orig[name][0] + orig[name][1]] for name in _NEW_ORDER]
    parts.append(jnp.zeros((w_in.shape[0], _N_PAD), w_in.dtype))
    return jnp.concatenate(parts, axis=1).astype(BF16)


def _trunk(x, seq_lens, norm_g, w_in, lru_conv_w, lru_conv_b, lru_w_r, lru_b_r, lru_w_i, lru_b_i, lru_lambda,
           rwkv_mu, rwkv_w0, rwkv_w2, rwkv_a0, rwkv_a2, rwkv_k_k, rwkv_k_a, rwkv_r_k, rwkv_lnx_g, rwkv_lnx_b,
           ret_gn_g, s5_lam_re, s5_lam_im, s5_log_step, s5_b_re, s5_b_im, s5_c_re, s5_c_im, s5_d, s5_glu_w,
           s5_glu_b, w_branch, w_out, final_g):
    n = x.shape[0]
    tp = _tile_plan(n, seq_lens)
    flags = _seq_flags(seq_lens, tp["sweep"])
    flags_rw = _seq_flags(seq_lens, tp["rwkv"])
    ret_tabs = _ret_tables(seq_lens)
    depth = w_in.shape[0]
    h = _norm_call(x, norm_g[0].reshape(1, D_MODEL), tp["norm_m"])
    for l in range(depth):
        proj = _inproj_call(h, _permute_w_in(w_in[l]), tp["in_m"], tp["in_n"])
        lru_h = _lru_call(proj, *_prep_lru(lru_conv_w[l], lru_conv_b[l], lru_w_r[l], lru_b_r[l], lru_w_i[l],
                                           lru_b_i[l], lru_lambda[l]), flags, tp["sweep"])
        rw_prm = _prep_rwkv(rwkv_mu[l], rwkv_w0[l], rwkv_w2[l], rwkv_a0[l], rwkv_a2[l], rwkv_k_k[l], rwkv_k_a[l],
                            rwkv_r_k[l])
        rw_y, rw_bonus = _rwkv_call(proj, *rw_prm, flags_rw, tp["rwkv"])
        ret_o = _ret_call(proj, *ret_tabs, flags, tp["sweep"])
        s5_y = _s5_call(proj, *_prep_s5(s5_lam_re[l], s5_lam_im[l], s5_log_step[l], s5_b_re[l], s5_b_im[l],
                                        s5_c_re[l], s5_c_im[l], tp["sweep"]), flags, tp["sweep"])
        fprm = jnp.stack([rwkv_lnx_g[l], rwkv_lnx_b[l], ret_gn_g[l], s5_d[l], s5_glu_b[l],
                          jnp.zeros_like(s5_d[l]), jnp.zeros_like(s5_d[l]), jnp.zeros_like(s5_d[l])])
        ycat = _finalize_call(proj, lru_h, rw_y, rw_bonus, ret_o, s5_y, fprm, s5_glu_w[l].astype(BF16),
                              rw_prm[4], tp["fin_m"])
        merged = _merge_call(ycat, proj, w_branch[l].astype(BF16), tp["merge_m"], tp["merge_n"])
        last = l == depth - 1
        g_next = (final_g if last else norm_g[l + 1]).reshape(1, D_MODEL)
        res = _outproj_call(x, merged, w_out[l].astype(BF16), g_next, tp["out_m"], last)
        if last:
            return res
        x, h = res


def kernel(x_prompt, x_sample, norm_g, w_in, lru_conv_w, lru_conv_b, lru_w_r, lru_b_r, lru_w_i, lru_b_i, lru_lambda, rwkv_mu, rwkv_w0, rwkv_w2, rwkv_a0, rwkv_a2, rwkv_k_k, rwkv_k_a, rwkv_r_k, rwkv_lnx_g, rwkv_lnx_b, ret_gn_g, s5_lam_re, s5_lam_im, s5_log_step, s5_b_re, s5_b_im, s5_c_re, s5_c_im, s5_d, s5_glu_w, s5_glu_b, w_branch, w_out, final_g):
    bp, lp, d = x_prompt.shape
    bs, ls, _ = x_sample.shape
    seq_lens = (lp,) * bp + (ls,) * bs
    x = jnp.concatenate([x_prompt.reshape(bp * lp, d), x_sample.reshape(bs * ls, d)], axis=0)
    y = _trunk(x, seq_lens, norm_g, w_in, lru_conv_w, lru_conv_b, lru_w_r, lru_b_r, lru_w_i, lru_b_i, lru_lambda,
               rwkv_mu, rwkv_w0, rwkv_w2, rwkv_a0, rwkv_a2, rwkv_k_k, rwkv_k_a, rwkv_r_k, rwkv_lnx_g, rwkv_lnx_b,
               ret_gn_g, s5_lam_re, s5_lam_im, s5_log_step, s5_b_re, s5_b_im, s5_c_re, s5_c_im, s5_d, s5_glu_w,
               s5_glu_b, w_branch, w_out, final_g)
    return y[:bp * lp].reshape(bp, lp, d), y[bp * lp:].reshape(bs, ls, d)
```

```python
import functools
import math

import numpy as np
import jax
import jax.numpy as jnp
from jax import lax
from jax.experimental import pallas as pl
from jax.experimental.pallas import tpu as pltpu

F32 = jnp.float32
BF16 = jnp.bfloat16
SWEEP_OUT = BF16

D_MODEL = 2048
DEPTH = 4
W_BR = D_MODEL // 2
N_BRANCH = 4
LRU_BLOCKS = 8
LRU_BS = W_BR // LRU_BLOCKS
LRU_C = 8.0
RWKV_HEAD = 64
RWKV_H = W_BR // RWKV_HEAD
RWKV_LORA = 64
RWKV_LN_EPS = RWKV_HEAD * 1e-5
RWKV_CHUNK = 64
RET_H = 4
RET_DK = W_BR // 2 // RET_H
RET_DV = W_BR // RET_H
RET_CHUNK = 128
RET_GN_EPS = 1e-5
ROPE_BASE = 10000.0
S5_P = 16
S5_G = W_BR // S5_P
S5_N = 64
S5_GB = 8
S5_NB = S5_G // S5_GB
S5_SW = S5_GB * S5_N
LAM_RE_MAX = -1e-4
NORM_EPS = 1e-6

SUBLANES = 8
LANES = 128
VMEM_LIMIT = 56 * 1024 * 1024

_ORIG_SPLITS = (
    ("lru_x", W_BR), ("lru_z", W_BR),
    ("rw_r", W_BR), ("rw_k", W_BR), ("rw_v", W_BR), ("rw_wdf", RWKV_LORA), ("rw_wdb", RWKV_LORA),
    ("rw_ad", RWKV_LORA), ("rw_z", W_BR),
    ("ret_q", RET_H * RET_DK), ("ret_k", RET_H * RET_DK), ("ret_v", W_BR), ("ret_z", W_BR),
    ("s5_u", W_BR), ("s5_z", W_BR),
    ("gates", N_BRANCH * D_MODEL),
)
_NEW_ORDER = ("lru_x", "lru_z", "rw_r", "rw_k", "rw_v", "rw_z", "ret_v", "ret_z", "s5_u", "s5_z", "gates",
              "ret_q", "ret_k", "rw_wdf", "rw_wdb", "rw_ad")


def _column_layout():
    orig, start = {}, 0
    for name, n in _ORIG_SPLITS:
        orig[name] = (start, n)
        start += n
    new, perm, pos = {}, [], 0
    for name in _NEW_ORDER:
        s, n = orig[name]
        new[name] = pos
        perm.extend(range(s, s + n))
        pos += n
    pad = (-pos) % 256
    return new, np.asarray(perm, np.int32), pos, pad


COL, _PERM, _N_IN, _N_PAD = _column_layout()
N_COLS = _N_IN + _N_PAD


def _cparams(sem):
    return pltpu.CompilerParams(dimension_semantics=sem, vmem_limit_bytes=VMEM_LIMIT)


def _sigmoid(x):
    return 1.0 / (1.0 + jnp.exp(-x))


def _softplus(x):
    return jnp.maximum(x, 0.0) + jnp.log1p(jnp.exp(-jnp.abs(x)))


def _one_minus_exp(x):
    series = -x * (1.0 + x * (0.5 + x * (1.0 / 6.0 + x * (1.0 / 24.0))))
    return jnp.where(x > -0.03, series, 1.0 - jnp.exp(x))


def _silu(x):
    return x * _sigmoid(x)


def _bdot(a, b):
    return jnp.dot(a.astype(BF16), b.astype(BF16), preferred_element_type=F32)


def _split_dot(x, w):
    hi = x.astype(BF16)
    lo = (x - hi.astype(F32)).astype(BF16)
    return (jnp.dot(hi, w, preferred_element_type=F32) + jnp.dot(lo, w, preferred_element_type=F32))


def _rmsnorm_rows(x, g):
    return x * lax.rsqrt(jnp.mean(x * x, axis=-1, keepdims=True) + NORM_EPS) * g


def _norm_call(x, g, tm):
    n, d = x.shape

    def body(x_ref, g_ref, o_ref):
        o_ref[...] = _rmsnorm_rows(x_ref[...], g_ref[...]).astype(BF16)

    return pl.pallas_call(
        body, grid=(n // tm,),
        in_specs=[pl.BlockSpec((tm, d), lambda i: (i, 0)), pl.BlockSpec((1, d), lambda i: (0, 0))],
        out_specs=pl.BlockSpec((tm, d), lambda i: (i, 0)),
        out_shape=jax.ShapeDtypeStruct((n, d), BF16),
        compiler_params=_cparams(("parallel",)), name="rmsnorm")(x, g)


def _inproj_call(h, w, tm, tn):
    n, k = h.shape
    nc = w.shape[1]

    def body(h_ref, w_ref, o_ref):
        o_ref[...] = jnp.dot(h_ref[...], w_ref[...], preferred_element_type=F32)

    return pl.pallas_call(
        body, grid=(nc // tn, n // tm),
        in_specs=[pl.BlockSpec((tm, k), lambda j, i: (i, 0)), pl.BlockSpec((k, tn), lambda j, i: (0, j))],
        out_specs=pl.BlockSpec((tm, tn), lambda j, i: (i, j)),
        out_shape=jax.ShapeDtypeStruct((n, nc), F32),
        compiler_params=_cparams(("parallel", "arbitrary")), name="inproj")(h, w)


def _outproj_call(x, merged, w_out, g, tm, last):
    n, d = x.shape

    def body(x_ref, m_ref, w_ref, g_ref, *o_refs):
        y = x_ref[...] + jnp.dot(m_ref[...], w_ref[...], preferred_element_type=F32)
        hn = _rmsnorm_rows(y, g_ref[...])
        if last:
            o_refs[0][...] = hn
        else:
            o_refs[0][...] = y
            o_refs[1][...] = hn.astype(BF16)

    row = pl.BlockSpec((tm, d), lambda i: (i, 0))
    if last:
        out_specs, out_shape = row, jax.ShapeDtypeStruct((n, d), F32)
    else:
        out_specs = (row, row)
        out_shape = (jax.ShapeDtypeStruct((n, d), F32), jax.ShapeDtypeStruct((n, d), BF16))
    return pl.pallas_call(
        body, grid=(n // tm,),
        in_specs=[row, row, pl.BlockSpec((d, d), lambda i: (0, 0)), pl.BlockSpec((1, d), lambda i: (0, 0))],
        out_specs=out_specs, out_shape=out_shape,
        compiler_params=_cparams(("parallel",)), name="outproj")(x, merged, w_out, g)


def _gelu_tanh(x):
    return 0.5 * x * (1.0 + jnp.tanh(math.sqrt(2.0 / math.pi) * (x + 0.044715 * (x * x * x))))


def _finalize_call(proj, lru_h, rw_y, rw_bonus, ret_o, s5_y, fprm, glu_w, ones_bd, tm):
    n = proj.shape[0]
    w = W_BR

    def body(lz_ref, rz_ref, cz_ref, su_ref, sz_ref, lh_ref, ry_ref, rb_ref, co_ref, sy_ref, p_ref, glu_ref,
             ones_ref, o_ref):
        def both(ref, sl=slice(None)):
            return ref[0, :, sl].astype(F32) + ref[1, :, sl].astype(F32)

        o_ref[0] = (both(lh_ref) * _silu(lz_ref[...])).astype(BF16)

        y = both(ry_ref)
        ones = ones_ref[...]
        yc = y - _split_dot(y, ones) * (1.0 / RWKV_HEAD)
        var = _split_dot(yc * yc, ones) * (1.0 / RWKV_HEAD)
        yb = yc * lax.rsqrt(var + RWKV_LN_EPS) * p_ref[0:1, :] + p_ref[1:2, :] + rb_ref[...].astype(F32)
        o_ref[1] = (yb * _silu(rz_ref[...])).astype(BF16)

        for h in range(RET_H):
            sl = slice(RET_DV * h, RET_DV * (h + 1))
            oh = both(co_ref, sl)
            oc = oh - jnp.mean(oh, axis=-1, keepdims=True)
            ov = jnp.mean(oc * oc, axis=-1, keepdims=True)
            yc_h = oc * lax.rsqrt(ov + RET_GN_EPS) * p_ref[2:3, sl]
            o_ref[2, :, sl] = (yc_h * _silu(cz_ref[:, sl])).astype(BF16)

        s = _gelu_tanh(both(sy_ref) + p_ref[3:4, :] * su_ref[...])
        s = s * _sigmoid(_bdot(s, glu_ref[...]) + p_ref[4:5, :])
        o_ref[3] = (s * _silu(sz_ref[...])).astype(BF16)

    def col(name):
        cb = COL[name] // w
        return pl.BlockSpec((tm, w), lambda i: (i, cb))

    both = pl.BlockSpec((2, tm, w), lambda i: (0, i, 0))
    first = pl.BlockSpec((None, tm, w), lambda i: (0, i, 0))

    def const(shape):
        return pl.BlockSpec(tuple(shape), lambda i: (0,) * len(shape))

    return pl.pallas_call(
        body, grid=(n // tm,),
        in_specs=[col("lru_z"), col("rw_z"), col("ret_z"), col("s5_u"), col("s5_z"), both, both, first, both, both,
                  const(fprm.shape), const(glu_w.shape), const(ones_bd.shape)],
        out_specs=pl.BlockSpec((N_BRANCH, tm, w), lambda i: (0, i, 0)),
        out_shape=jax.ShapeDtypeStruct((N_BRANCH, n, w), BF16),
        compiler_params=_cparams(("parallel",)), name="finalize")(
            proj, proj, proj, proj, proj, lru_h, rw_y, rw_bonus, ret_o, s5_y, fprm, glu_w, ones_bd)


def _merge_call(ycat, proj, w_branch, tm, tn):
    n = proj.shape[0]
    gate0 = COL["gates"] // tn
    per_branch = D_MODEL // tn

    def body(y_ref, g0, g1, g2, g3, w_ref, o_ref):
        acc = None
        for b, g_ref in enumerate((g0, g1, g2, g3)):
            term = _sigmoid(g_ref[...]) * jnp.dot(y_ref[b], w_ref[b], preferred_element_type=F32)
            acc = term if acc is None else acc + term
        o_ref[...] = acc.astype(BF16)

    gates = [pl.BlockSpec((tm, tn), lambda j, i, b=b: (i, gate0 + b * per_branch + j)) for b in range(N_BRANCH)]
    return pl.pallas_call(
        body, grid=(D_MODEL // tn, n // tm),
        in_specs=[pl.BlockSpec((N_BRANCH, tm, W_BR), lambda j, i: (0, i, 0)), *gates,
                  pl.BlockSpec((N_BRANCH, W_BR, tn), lambda j, i: (0, 0, j))],
        out_specs=pl.BlockSpec((tm, tn), lambda j, i: (i, j)),
        out_shape=jax.ShapeDtypeStruct((n, D_MODEL), BF16),
        compiler_params=_cparams(("parallel", "arbitrary")), name="merge")(
            ycat, proj, proj, proj, proj, w_branch)


def _seq_flags(seq_lens, t):
    sf, ef = [], []
    for ln in seq_lens:
        assert ln % t == 0
        k = ln // t
        sf += [1] + [0] * (k - 1)
        ef += [0] * (k - 1) + [1]
    return jnp.asarray(sf, jnp.int32), jnp.asarray(ef, jnp.int32)


def _tile_of(d, i, n_t):
    return i + d * (n_t - 1 - 2 * i)


def _tile_spec(t, w, col, n_t):
    cb = col // w
    return pl.BlockSpec((t, w), lambda d, i, sf, ef: (_tile_of(d, i, n_t), cb))


def _halo_specs(t, w, col, n_t):
    cb = col // w
    hb = t // SUBLANES
    last = n_t * hb - 1
    prev = pl.BlockSpec((SUBLANES, w), lambda d, i, sf, ef: (jnp.maximum(_tile_of(d, i, n_t) * hb - 1, 0), cb))
    nxt = pl.BlockSpec((SUBLANES, w), lambda d, i, sf, ef: (jnp.minimum((_tile_of(d, i, n_t) + 1) * hb, last), cb))
    return prev, nxt


def _dir_spec(shape):
    nd = len(shape)
    return pl.BlockSpec((None,) + tuple(shape[1:]), lambda d, i, sf, ef: (d,) + (0,) * (nd - 1))


def _const_spec(shape):
    nd = len(shape)
    return pl.BlockSpec(tuple(shape), lambda d, i, sf, ef: (0,) * nd)


def _fill_halo(buf, x_ref, xp_ref, xn_ref, keep_p, keep_n, t):
    buf[0:SUBLANES, :] = xp_ref[...] * keep_p
    buf[SUBLANES:t + SUBLANES, :] = x_ref[...]
    buf[t + SUBLANES:t + 2 * SUBLANES, :] = xn_ref[...] * keep_n


def _seg_pitch(t):
    return t // SUBLANES + 4


def _seg_rows(t):
    return SUBLANES * _seg_pitch(t)


def _to_segments(ref, blk, val, t):
    s_len, pitch = t // SUBLANES, _seg_pitch(t)
    for k in range(SUBLANES):
        ref[blk, pitch * k:pitch * k + s_len, :] = val[s_len * k:s_len * (k + 1), :]


def _from_segments(ref, blk, t):
    s_len, pitch = t // SUBLANES, _seg_pitch(t)
    return jnp.concatenate([ref[blk, pitch * k:pitch * k + s_len, :] for k in range(SUBLANES)], axis=0)


def _seg_scan_real(a_ref, b_ref, o_ref, carry_ref, t, rev):
    s_len = t // SUBLANES
    pitch = _seg_pitch(t)
    nb = a_ref.shape[0]
    blocks = range(nb)

    def ld(ref, cb, s):
        return ref[cb, pl.ds(s, SUBLANES, stride=pitch), :]

    def step_of(j):
        return s_len - 1 - j if rev else j

    def pass1(j, hp):
        hs, ps = hp
        s = step_of(j)
        a = [ld(a_ref, cb, s) for cb in blocks]
        return (tuple(a[cb] * hs[cb] + ld(b_ref, cb, s) for cb in blocks),
                tuple(a[cb] * ps[cb] for cb in blocks))

    zero = jnp.zeros((SUBLANES, LANES), F32)
    es, ps = lax.fori_loop(0, s_len, pass1, ((zero,) * nb, (zero + 1.0,) * nb), unroll=2)
    starts = []
    for cb in blocks:
        c = carry_ref[cb, 0:1, :]
        rows = [None] * SUBLANES
        for k in (range(SUBLANES - 1, -1, -1) if rev else range(SUBLANES)):
            rows[k] = c
            c = ps[cb][k:k + 1, :] * c + es[cb][k:k + 1, :]
        carry_ref[cb, 0:1, :] = c
        starts.append(jnp.concatenate(rows, axis=0))

    def pass2(j, hs):
        s = step_of(j)
        out = []
        for cb in blocks:
            h = ld(a_ref, cb, s) * hs[cb] + ld(b_ref, cb, s)
            o_ref[cb, pl.ds(s, SUBLANES, stride=pitch), :] = h
            out.append(h)
        return tuple(out)

    lax.fori_loop(0, s_len, pass2, tuple(starts), unroll=2)


def _lru_call(proj, conv_w, conv_b, wcat, bcat, sp, flags, t):
    n = proj.shape[0]
    n_t = n // t
    w = W_BR

    def body(sf_ref, ef_ref, x_ref, xp_ref, xn_ref, cw_ref, cb_ref, w_ref, b_ref, sp_ref, o_ref,
             xbuf, a_scr, b_scr, h_scr, carry):
        d = pl.program_id(0)
        ti = _tile_of(d, pl.program_id(1), n_t)
        sf = sf_ref[ti]
        ef = ef_ref[ti]
        _fill_halo(xbuf, x_ref, xp_ref, xn_ref, (1 - sf).astype(F32), (1 - ef).astype(F32), t)
        o = SUBLANES
        xc = (cw_ref[0:1, :] * xbuf[o - 2:o - 2 + t, :] + cw_ref[1:2, :] * xbuf[o - 1:o - 1 + t, :]
              + cw_ref[2:3, :] * xbuf[o:o + t, :] + cw_ref[3:4, :] * xbuf[o + 1:o + 1 + t, :] + cb_ref[...])
        row = lax.broadcasted_iota(jnp.int32, (t, LRU_BS), 0)
        first_row = jnp.where(d == 0, 0, t - 1)
        at_seq_edge = jnp.where(d == 0, sf, ef) == 1
        first = jnp.logical_and(row == first_row, at_seq_edge)
        for hb in range(LRU_BLOCKS):
            sl = slice(LRU_BS * hb, LRU_BS * (hb + 1))
            xcb = xc[:, sl]
            g = _bdot(xcb, w_ref[hb])
            r = _sigmoid(g[:, :LRU_BS] + b_ref[0:1, sl])
            ig = _sigmoid(g[:, LRU_BS:] + b_ref[1:2, sl])
            log_a = (-LRU_C) * r * _softplus(-sp_ref[0:1, sl])
            mult = jnp.sqrt(_one_minus_exp(2.0 * log_a))
            mult = jnp.where(first, 1.0, mult)
            _to_segments(a_scr, hb, jnp.exp(log_a), t)
            _to_segments(b_scr, hb, mult * ig * xcb, t)

        @pl.when(at_seq_edge)
        def _():
            carry[...] = jnp.zeros_like(carry)

        @pl.when(d == 0)
        def _():
            _seg_scan_real(a_scr, b_scr, h_scr, carry, t, False)

        @pl.when(d == 1)
        def _():
            _seg_scan_real(a_scr, b_scr, h_scr, carry, t, True)

        for hb in range(LRU_BLOCKS):
            o_ref[:, LRU_BS * hb:LRU_BS * (hb + 1)] = _from_segments(h_scr, hb, t).astype(SWEEP_OUT)

    prev, nxt = _halo_specs(t, w, COL["lru_x"], n_t)
    grid_spec = pltpu.PrefetchScalarGridSpec(
        num_scalar_prefetch=2, grid=(2, n_t),
        in_specs=[_tile_spec(t, w, COL["lru_x"], n_t), prev, nxt,
                  _const_spec(conv_w.shape), _const_spec(conv_b.shape),
                  _dir_spec(wcat.shape), _dir_spec(bcat.shape), _dir_spec(sp.shape)],
        out_specs=pl.BlockSpec((None, t, w), lambda d, i, sf, ef: (d, _tile_of(d, i, n_t), 0)),
        scratch_shapes=[pltpu.VMEM((t + 2 * SUBLANES, w), F32)]
        + [pltpu.VMEM((LRU_BLOCKS, _seg_rows(t), LRU_BS), F32)] * 3
        + [pltpu.VMEM((LRU_BLOCKS, SUBLANES, LRU_BS), F32)])
    return pl.pallas_call(
        body, grid_spec=grid_spec, out_shape=jax.ShapeDtypeStruct((2, n, w), SWEEP_OUT),
        compiler_params=_cparams(("arbitrary", "arbitrary")), name="lru_sweep")(
            *flags, proj, proj, proj, conv_w, conv_b, wcat, bcat, sp)


def _prep_lru(conv_w, conv_b, w_r, b_r, w_i, b_i, lam):
    wcat = jnp.concatenate([w_r, w_i], axis=-1).astype(BF16)
    bcat = jnp.stack([b_r, b_i], axis=1)
    return conv_w, conv_b.reshape(1, W_BR), wcat, bcat, lam.reshape(2, 1, W_BR)


_NT3 = (((2,), (2,)), ((0,), (0,)))
_NN3 = (((2,), (1,)), ((0,), (0,)))


def _nt(a, b):
    return lax.dot_general(a.astype(BF16), b.astype(BF16), _NT3, preferred_element_type=F32)


def _nn(a, b):
    return lax.dot_general(a.astype(BF16), b.astype(BF16), _NN3, preferred_element_type=F32)


def _rwkv_call(proj, prm, w0, w2, a2, ones_bd, tri, flags, t):
    n = proj.shape[0]
    n_t = n // t
    w = W_BR
    c = RWKV_CHUNK
    n_c = t // c
    nh, hd = RWKV_H, RWKV_HEAD

    def body(sf_ref, ef_ref, r_ref, rp_ref, rn_ref, k_ref, kp_ref, kn_ref, v_ref, vp_ref, vn_ref, lora_ref,
             prm_ref, w0_ref, w2_ref, a2_ref, ones_ref, tri_ref, y_ref, bonus_ref,
             xbuf, art_s, bk_s, vt_s, top_s, bot_s, tinv_s, yt_s, gam_s, state):
        d = pl.program_id(0)
        ti = _tile_of(d, pl.program_id(1), n_t)
        sf = sf_ref[ti]
        ef = ef_ref[ti]
        keep_p, keep_n = (1 - sf).astype(F32), (1 - ef).astype(F32)
        o = SUBLANES

        def mixed(x_ref, xp_ref, xn_ref, mu):
            _fill_halo(xbuf, x_ref, xp_ref, xn_ref, keep_p, keep_n, t)
            x = xbuf[o:o + t, :]
            return x + mu * (0.5 * (xbuf[o - 1:o - 1 + t, :] + xbuf[o + 1:o + 1 + t, :]) - x)

        r = mixed(r_ref, rp_ref, rn_ref, prm_ref[0:1, :])
        k = mixed(k_ref, kp_ref, kn_ref, prm_ref[1:2, :])
        v = mixed(v_ref, vp_ref, vn_ref, prm_ref[2:3, :])
        lora = lora_ref[...]
        wd = jnp.where(d == 0, lora[:, 0:RWKV_LORA], lora[:, RWKV_LORA:2 * RWKV_LORA])
        ad = lora[:, 2 * RWKV_LORA:3 * RWKV_LORA]
        a_icl = _sigmoid(prm_ref[3:4, :] + _bdot(ad, a2_ref[...]))
        w_log = -_softplus(-(w0_ref[...] + _bdot(jnp.tanh(wd), w2_ref[...]))) - 0.5
        logw = -jnp.exp(w_log)
        kkr = k * prm_ref[4:5, :]
        kk = kkr / jnp.maximum(jnp.sqrt(_bdot(kkr * kkr, ones_ref[...])), 1e-12)
        k_mod = k * (1.0 + (a_icl - 1.0) * prm_ref[5:6, :])
        bonus_ref[...] = (_split_dot(r * k_mod * prm_ref[6:7, :], ones_ref[...]) * v).astype(SWEEP_OUT)
        b_vec = kk * a_icl

        ii = lax.broadcasted_iota(jnp.int32, (c, 2 * c), 0)
        jj = lax.broadcasted_iota(jnp.int32, (c, 2 * c), 1)
        jj = jnp.where(jj >= c, jj - c, jj)
        order = (1 - 2 * d) * (ii - jj)
        strict = (order > 0)[None]
        incl = (order >= 0)[None]
        eye = (lax.broadcasted_iota(jnp.int32, (c, c), 0)
               == lax.broadcasted_iota(jnp.int32, (c, c), 1)).astype(F32)[None]

        vt3 = v.T.reshape(nh, hd, t)
        tri_m = tri_ref[...]
        for cc in range(n_c):
            rows = slice(c * cc, c * (cc + 1))
            lw = logw[rows, :]
            hi = lw.astype(BF16)
            lo = (lw - hi.astype(F32)).astype(BF16)
            g = jnp.dot(tri_m, hi, preferred_element_type=F32) + jnp.dot(tri_m, lo, preferred_element_type=F32)
            e_pos = jnp.exp(g)
            e_neg = jnp.exp(-g)
            at_c = -kk[rows, :] * jnp.exp(g - lw)
            rt_c = r[rows, :] * e_pos
            bt_c = b_vec[rows, :] * e_neg
            kt_c = k_mod[rows, :] * e_neg
            g_end = jnp.exp(jnp.where(d == 0, g[c - 1:c, :], g[0:1, :]))
            for h in range(nh):
                ls = slice(hd * h, hd * (h + 1))
                art_s[cc, h, 0:c, :] = at_c[:, ls]
                art_s[cc, h, c:2 * c, :] = rt_c[:, ls]
                bk_s[cc, h, 0:c, :] = bt_c[:, ls]
                bk_s[cc, h, c:2 * c, :] = kt_c[:, ls]
                gam_s[h, SUBLANES * cc:SUBLANES * (cc + 1), :] = jnp.broadcast_to(g_end[:, ls], (SUBLANES, hd))
            vt_c = vt3[:, :, rows]
            vt_s[cc] = jnp.concatenate([jnp.zeros_like(vt_c), vt_c], axis=2)

        @pl.when(jnp.where(d == 0, sf, ef) == 1)
        def _():
            state[...] = jnp.zeros_like(state)

        def chunk_of(j):
            return jnp.where(d == 0, j, n_c - 1 - j)

        def prepare(p, carry):
            pair = pl.ds(pl.multiple_of(2 * p, 2), 2)
            a_all = _nt(art_s[pair].reshape(2 * nh, 2 * c, hd), bk_s[pair].reshape(2 * nh, 2 * c, hd))
            top = jnp.where(strict, a_all[:, :c], 0.0)
            a_ab = top[:, :, :c]
            tinv = eye + a_ab
            pw = a_ab
            for _ in range(int(math.log2(c)) - 1):
                pw = _nn(pw, pw)
                tinv = tinv + _nn(pw, tinv)
            top_s[pair] = top.reshape(2, nh, c, 2 * c)
            bot_s[pair] = jnp.where(incl, a_all[:, c:], 0.0).reshape(2, nh, c, 2 * c)
            tinv_s[pair] = tinv.reshape(2, nh, c, c)
            return carry

        def advance(j, carry):
            cc = chunk_of(j)
            art = art_s[cc]
            bk = bk_s[cc]
            at, rt = art[:, :c], art[:, c:]
            vt_hi = vt_s[cc]
            gam = gam_s[:, pl.ds(pl.multiple_of(cc * SUBLANES, SUBLANES), 1), :]
            s0 = state[...]
            zt = _nt(s0, at) + _nt(vt_hi, top_s[cc])
            ut = _nt(zt, tinv_s[cc])
            ut_vt = jnp.concatenate([ut, jnp.zeros_like(ut)], axis=2) + vt_hi
            yt_s[cc] = _nt(s0, rt) + _nt(ut_vt, bot_s[cc])
            state[...] = (s0 + _nn(ut_vt, bk)) * gam
            return carry

        lax.fori_loop(0, n_c // 2, prepare, 0)
        lax.fori_loop(0, n_c, advance, 0)
        yt = jnp.concatenate([yt_s[cc] for cc in range(n_c)], axis=2)
        y_ref[...] = yt.reshape(w, t).T.astype(SWEEP_OUT)

    lora_col = COL["rw_wdf"]
    r_halo = _halo_specs(t, w, COL["rw_r"], n_t)
    k_halo = _halo_specs(t, w, COL["rw_k"], n_t)
    v_halo = _halo_specs(t, w, COL["rw_v"], n_t)
    stacked = pltpu.VMEM((n_c, nh, 2 * c, hd), F32)
    paired = pltpu.VMEM((n_c, nh, hd, 2 * c), F32)
    grid_spec = pltpu.PrefetchScalarGridSpec(
        num_scalar_prefetch=2, grid=(2, n_t),
        in_specs=[_tile_spec(t, w, COL["rw_r"], n_t), *r_halo, _tile_spec(t, w, COL["rw_k"], n_t), *k_halo,
                  _tile_spec(t, w, COL["rw_v"], n_t), *v_halo, _tile_spec(t, 4 * RWKV_LORA, lora_col, n_t),
                  _const_spec(prm.shape), _dir_spec(w0.shape), _dir_spec(w2.shape), _const_spec(a2.shape),
                  _const_spec(ones_bd.shape), _dir_spec(tri.shape)],
        out_specs=(pl.BlockSpec((None, t, w), lambda d, i, sf, ef: (d, _tile_of(d, i, n_t), 0)),
                   pl.BlockSpec((None, t, w), lambda d, i, sf, ef: (d, _tile_of(d, i, n_t), 0))),
        scratch_shapes=[pltpu.VMEM((t + 2 * SUBLANES, w), F32), stacked, stacked,
                        paired, paired, paired, pltpu.VMEM((n_c, nh, c, c), F32),
                        pltpu.VMEM((n_c, nh, hd, c), F32),
                        pltpu.VMEM((nh, n_c * SUBLANES, hd), F32),
                        pltpu.VMEM((nh, hd, hd), F32)])
    return pl.pallas_call(
        body, grid_spec=grid_spec,
        out_shape=(jax.ShapeDtypeStruct((2, n, w), SWEEP_OUT), jax.ShapeDtypeStruct((2, n, w), SWEEP_OUT)),
        compiler_params=_cparams(("arbitrary", "arbitrary")), name="rwkv_sweep")(
            *flags, proj, proj, proj, proj, proj, proj, proj, proj, proj, proj, prm, w0, w2, a2, ones_bd, tri)


def _prep_rwkv(mu, w0, w2, a0, a2, k_k, k_a, r_k):
    prm = jnp.concatenate([mu, a0[None], k_k[None], k_a[None], r_k.reshape(1, W_BR),
                           jnp.zeros((1, W_BR), F32)], axis=0)
    head = np.arange(W_BR) // RWKV_HEAD
    ones_bd = jnp.asarray(head[:, None] == head[None, :], BF16)
    i = np.arange(RWKV_CHUNK)
    tri = jnp.asarray(np.stack([i[:, None] >= i[None, :], i[:, None] <= i[None, :]]), BF16)
    return prm, w0.reshape(2, 1, W_BR), w2.astype(BF16), a2.astype(BF16), ones_bd, tri


def _head_norm_ref(y, eps, hd):
    yh = y.reshape(y.shape[0], -1, hd)
    mean = jnp.mean(yh, axis=-1, keepdims=True)
    yc = yh - mean
    var = jnp.mean(yc * yc, axis=-1, keepdims=True)
    return (yc * lax.rsqrt(var + eps)).reshape(y.shape)


def _ret_tables(seq_lens):
    half = RET_DK // 2
    pos = jnp.concatenate([jnp.arange(ln, dtype=F32) for ln in seq_lens])
    inv = ROPE_BASE ** (-jnp.arange(half, dtype=F32) / half)
    ang = pos[:, None] * inv[None, :]
    cos, sin = jnp.cos(ang), jnp.sin(ang)
    cos_t = jnp.concatenate([cos, cos], axis=1)
    sin_t = jnp.concatenate([-sin, sin], axis=1)
    c = RET_CHUNK
    log_g = np.log(1.0 - 2.0 ** (-5.0 - np.arange(RET_H, dtype=np.float64)))
    i = np.arange(c, dtype=np.float64)
    dmat = np.exp(log_g[:, None, None] * np.abs(i[:, None] - i[None, :]))
    rows = np.stack([np.exp(log_g[:, None] * (i + 1.0)), np.exp(log_g[:, None] * (c - i)),
                     np.exp(log_g[:, None] * (c - 1.0 - i)), np.exp(log_g[:, None] * i)], axis=1)
    rows = np.broadcast_to(rows[..., None], (RET_H, 4, c, RET_DK))
    return cos_t, sin_t, jnp.asarray(dmat, F32), jnp.asarray(rows, F32)


def _ret_call(proj, cos_t, sin_t, dmat, rowsc, flags, t):
    n = proj.shape[0]
    n_t = n // t
    c = RET_CHUNK
    n_c = t // c
    g_chunk = [float((1.0 - 2.0 ** (-5.0 - h)) ** c) for h in range(RET_H)]
    tn_dims = (((0,), (0,)), ((), ()))
    nt_dims = (((1,), (1,)), ((), ()))

    def body(sf_ref, ef_ref, q_ref, k_ref, v_ref, cos_ref, sin_ref, dm_ref, rs_ref, o_ref, state):
        d = pl.program_id(0)
        ti = _tile_of(d, pl.program_id(1), n_t)
        at_seq_edge = jnp.where(d == 0, sf_ref[ti], ef_ref[ti]) == 1

        @pl.when(at_seq_edge)
        def _():
            state[...] = jnp.zeros_like(state)

        def rot(x_ref, rows, h):
            x = x_ref[rows, RET_DK * h:RET_DK * (h + 1)]
            return x * cos_ref[rows, :] + pltpu.roll(x, RET_DK // 2, 1) * sin_ref[rows, :]

        def chunk(cc, fwd):
            rows = slice(c * cc, c * (cc + 1))
            for h in range(RET_H):
                qh = rot(q_ref, rows, h)
                kh = rot(k_ref, rows, h) * (RET_DK ** -0.5)
                vh = v_ref[rows, RET_DV * h:RET_DV * (h + 1)].astype(BF16)
                s_old = state[h]
                if fwd:
                    sc = lax.dot_general(qh.astype(BF16), kh.astype(BF16), nt_dims,
                                         preferred_element_type=F32) * dm_ref[h]
                    o = (jnp.dot(sc.astype(BF16), vh, preferred_element_type=F32)
                         + _bdot(qh * rs_ref[h, 0], s_old))
                    kd = kh * rs_ref[h, 2]
                else:
                    o = _bdot(qh * rs_ref[h, 1], s_old)
                    kd = kh * rs_ref[h, 3]
                o_ref[rows, RET_DV * h:RET_DV * (h + 1)] = o.astype(SWEEP_OUT)
                state[h] = g_chunk[h] * s_old + lax.dot_general(kd.astype(BF16), vh, tn_dims,
                                                                 preferred_element_type=F32)

        @pl.when(d == 0)
        def _():
            for cc in range(n_c):
                chunk(cc, True)

        @pl.when(d == 1)
        def _():
            for cc in range(n_c - 1, -1, -1):
                chunk(cc, False)

    hk = RET_H * RET_DK
    tab = pl.BlockSpec((t, RET_DK), lambda d, i, sf, ef: (_tile_of(d, i, n_t), 0))
    grid_spec = pltpu.PrefetchScalarGridSpec(
        num_scalar_prefetch=2, grid=(2, n_t),
        in_specs=[_tile_spec(t, hk, COL["ret_q"], n_t), _tile_spec(t, hk, COL["ret_k"], n_t),
                  _tile_spec(t, W_BR, COL["ret_v"], n_t), tab, tab,
                  _const_spec(dmat.shape), _const_spec(rowsc.shape)],
        out_specs=pl.BlockSpec((None, t, W_BR), lambda d, i, sf, ef: (d, _tile_of(d, i, n_t), 0)),
        scratch_shapes=[pltpu.VMEM((RET_H, RET_DK, RET_DV), F32)])
    return pl.pallas_call(
        body, grid_spec=grid_spec, out_shape=jax.ShapeDtypeStruct((2, n, W_BR), SWEEP_OUT),
        compiler_params=_cparams(("arbitrary", "arbitrary")), name="ret_sweep")(
            *flags, proj, proj, proj, cos_t, sin_t, dmat, rowsc)


def _seg_scan_complex(br_ref, bi_ref, sr_ref, si_ref, carry_r, carry_i, base, lam_ref, t, rev):
    s_len = t // SUBLANES
    pitch = _seg_pitch(t)
    nb = br_ref.shape[0]
    blocks = range(nb)
    lr = [jnp.broadcast_to(lam_ref[0:1, LANES * cb:LANES * (cb + 1)], (SUBLANES, LANES)) for cb in blocks]
    li = [jnp.broadcast_to(lam_ref[1:2, LANES * cb:LANES * (cb + 1)], (SUBLANES, LANES)) for cb in blocks]

    def ld(ref, cb, s):
        return ref[cb, pl.ds(s, SUBLANES, stride=pitch), :]

    def step_of(j):
        return s_len - 1 - j if rev else j

    def advance(cb, s, hr, hi):
        nr = lr[cb] * hr - li[cb] * hi + ld(br_ref, cb, s)
        ni = lr[cb] * hi + li[cb] * hr + ld(bi_ref, cb, s)
        return nr, ni

    def pass1(j, h):
        s = step_of(j)
        return tuple(advance(cb, s, *h[cb]) for cb in blocks)

    zero = jnp.zeros((SUBLANES, LANES), F32)
    ends = lax.fori_loop(0, s_len, pass1, ((zero, zero),) * nb, unroll=2)
    starts = []
    for cb in blocks:
        sl = slice(LANES * cb, LANES * (cb + 1))
        pr, pi = lam_ref[2:3, sl], lam_ref[3:4, sl]
        cr, ci = carry_r[base + cb, 0:1, :], carry_i[base + cb, 0:1, :]
        rows_r, rows_i = [None] * SUBLANES, [None] * SUBLANES
        for k in (range(SUBLANES - 1, -1, -1) if rev else range(SUBLANES)):
            rows_r[k], rows_i[k] = cr, ci
            er, ei = ends[cb][0][k:k + 1, :], ends[cb][1][k:k + 1, :]
            cr, ci = pr * cr - pi * ci + er, pr * ci + pi * cr + ei
        carry_r[base + cb, 0:1, :] = cr
        carry_i[base + cb, 0:1, :] = ci
        starts.append((jnp.concatenate(rows_r, axis=0), jnp.concatenate(rows_i, axis=0)))

    def pass2(j, h):
        s = step_of(j)
        out = []
        for cb in blocks:
            nr, ni = advance(cb, s, *h[cb])
            sr_ref[cb, pl.ds(s, SUBLANES, stride=pitch), :] = nr
            si_ref[cb, pl.ds(s, SUBLANES, stride=pitch), :] = ni
            out.append((nr, ni))
        return tuple(out)

    lax.fori_loop(0, s_len, pass2, tuple(starts), unroll=2)


def _s5_call(proj, bblk, cblk, lam4, flags, t):
    n = proj.shape[0]
    n_t = n // t
    w = W_BR
    lb = S5_SW // LANES

    def body(sf_ref, ef_ref, u_ref, b_ref, c_ref, lam_ref, o_ref, br, bi, sr, si, carry_r, carry_i):
        d = pl.program_id(0)
        ti = _tile_of(d, pl.program_id(1), n_t)
        at_seq_edge = jnp.where(d == 0, sf_ref[ti], ef_ref[ti]) == 1

        @pl.when(at_seq_edge)
        def _():
            carry_r[...] = jnp.zeros_like(carry_r)
            carry_i[...] = jnp.zeros_like(carry_i)

        for ob in range(S5_NB):
            bu = _bdot(u_ref[:, LANES * ob:LANES * (ob + 1)], b_ref[ob])
            for cb in range(lb):
                _to_segments(br, cb, bu[:, LANES * cb:LANES * (cb + 1)], t)
                _to_segments(bi, cb, bu[:, S5_SW + LANES * cb:S5_SW + LANES * (cb + 1)], t)

            @pl.when(d == 0)
            def _():
                _seg_scan_complex(br, bi, sr, si, carry_r, carry_i, ob * lb, lam_ref.at[ob], t, False)

            @pl.when(d == 1)
            def _():
                _seg_scan_complex(br, bi, sr, si, carry_r, carry_i, ob * lb, lam_ref.at[ob], t, True)

            st = jnp.concatenate([_from_segments(sr, cb, t) for cb in range(lb)]
                                 + [_from_segments(si, cb, t) for cb in range(lb)], axis=1)
            o_ref[:, LANES * ob:LANES * (ob + 1)] = _bdot(st, c_ref[ob]).astype(SWEEP_OUT)

    grid_spec = pltpu.PrefetchScalarGridSpec(
        num_scalar_prefetch=2, grid=(2, n_t),
        in_specs=[_tile_spec(t, w, COL["s5_u"], n_t), _const_spec(bblk.shape), _dir_spec(cblk.shape),
                  _dir_spec(lam4.shape)],
        out_specs=pl.BlockSpec((None, t, w), lambda d, i, sf, ef: (d, _tile_of(d, i, n_t), 0)),
        scratch_shapes=[pltpu.VMEM((lb, _seg_rows(t), LANES), F32)] * 4
        + [pltpu.VMEM((S5_NB * lb, SUBLANES, LANES), F32)] * 2)
    return pl.pallas_call(
        body, grid_spec=grid_spec, out_shape=jax.ShapeDtypeStruct((2, n, w), SWEEP_OUT),
        compiler_params=_cparams(("arbitrary", "arbitrary")), name="s5_sweep")(
            *flags, proj, bblk, cblk, lam4)


def _prep_s5(lam_re, lam_im, log_step, b_re, b_im, c_re, c_im, t):
    s_len = t // SUBLANES
    assert s_len & (s_len - 1) == 0
    lre = jnp.minimum(lam_re, LAM_RE_MAX)
    step = jnp.exp(log_step)[..., None]
    mag = jnp.exp(lre * step)
    lbr, lbi = mag * jnp.cos(lam_im * step), mag * jnp.sin(lam_im * step)
    den = lre * lre + lam_im * lam_im
    gr = ((lbr - 1.0) * lre + lbi * lam_im) / den
    gi = (lbi * lre - (lbr - 1.0) * lam_im) / den
    pr, pi = lbr, lbi
    for _ in range(int(math.log2(s_len))):
        pr, pi = pr * pr - pi * pi, 2.0 * pr * pi
    lam4 = jnp.stack([x.reshape(2, S5_NB, S5_SW) for x in (lbr, lbi, pr, pi)], axis=2)
    cpr = c_re[None] * gr[:, :, None, :] - c_im[None] * gi[:, :, None, :]
    cpi = c_re[None] * gi[:, :, None, :] + c_im[None] * gr[:, :, None, :]
    eye = jnp.eye(S5_GB, dtype=F32)

    def c_rows(x):
        x = x.reshape(2, S5_NB, S5_GB, S5_P, S5_N)
        return jnp.einsum('dogpn,gh->dognhp', x, eye).reshape(2, S5_NB, S5_SW, S5_GB * S5_P)

    cblk = jnp.concatenate([c_rows(cpr), c_rows(-cpi)], axis=2).astype(BF16)

    def b_cols(x):
        x = x.reshape(S5_NB, S5_GB, S5_N, S5_P)
        return jnp.einsum('ognp,gh->ogphn', x, eye).reshape(S5_NB, S5_GB * S5_P, S5_SW)

    bblk = jnp.concatenate([b_cols(b_re), b_cols(b_im)], axis=2).astype(BF16)
    return bblk, cblk, lam4


def _tile_plan(n, seq_lens):
    shortest = min(seq_lens)
    return dict(
        sweep=min(512, shortest),
        rwkv=min(256, shortest),
        in_m=min(1024, n), in_n=1792,
        fin_m=min(256, n),
        merge_m=min(512, n), merge_n=1024,
        out_m=min(256, n),
        norm_m=min(512, n),
    )


def _permute_w_in(w_in):
    orig, start = {}, 0
    for name, nn in _ORIG_SPLITS:
        orig[name] = (start, nn)
        start += nn
    parts = [w_in[:, orig[name][0]:orig[name][0] + orig[name][1]] for name in _NEW_ORDER]
    parts.append(jnp.zeros((w_in.shape[0], _N_PAD), w_in.dtype))
    return jnp.concatenate(parts, axis=1).astype(BF16)


def _trunk(x, seq_lens, norm_g, w_in, lru_conv_w, lru_conv_b, lru_w_r, lru_b_r, lru_w_i, lru_b_i, lru_lambda,
           rwkv_mu, rwkv_w0, rwkv_w2, rwkv_a0, rwkv_a2, rwkv_k_k, rwkv_k_a, rwkv_r_k, rwkv_lnx_g, rwkv_lnx_b,
           ret_gn_g, s5_lam_re, s5_lam_im, s5_log_step, s5_b_re, s5_b_im, s5_c_re, s5_c_im, s5_d, s5_glu_w,
           s5_glu_b, w_branch, w_out, final_g):
    n = x.shape[0]
    tp = _tile_plan(n, seq_lens)
    flags = _seq_flags(seq_lens, tp["sweep"])
    flags_rw = _seq_flags(seq_lens, tp["rwkv"])
    ret_tabs = _ret_tables(seq_lens)
    depth = w_in.shape[0]
    h = _norm_call(x, norm_g[0].reshape(1, D_MODEL), tp["norm_m"])
    for l in range(depth):
        proj = _inproj_call(h, _permute_w_in(w_in[l]), tp["in_m"], tp["in_n"])
        lru_h = _lru_call(proj, *_prep_lru(lru_conv_w[l], lru_conv_b[l], lru_w_r[l], lru_b_r[l], lru_w_i[l],
                                           lru_b_i[l], lru_lambda[l]), flags, tp["sweep"])
        rw_prm = _prep_rwkv(rwkv_mu[l], rwkv_w0[l], rwkv_w2[l], rwkv_a0[l], rwkv_a2[l], rwkv_k_k[l], rwkv_k_a[l],
                            rwkv_r_k[l])
        rw_y, rw_bonus = _rwkv_call(proj, *rw_prm, flags_rw, tp["rwkv"])
        ret_o = _ret_call(proj, *ret_tabs, flags, tp["sweep"])
        s5_y = _s5_call(proj, *_prep_s5(s5_lam_re[l], s5_lam_im[l], s5_log_step[l], s5_b_re[l], s5_b_im[l],
                                        s5_c_re[l], s5_c_im[l], tp["sweep"]), flags, tp["sweep"])
        fprm = jnp.stack([rwkv_lnx_g[l], rwkv_lnx_b[l], ret_gn_g[l], s5_d[l], s5_glu_b[l],
                          jnp.zeros_like(s5_d[l]), jnp.zeros_like(s5_d[l]), jnp.zeros_like(s5_d[l])])
        ycat = _finalize_call(proj, lru_h, rw_y, rw_bonus, ret_o, s5_y, fprm, s5_glu_w[l].astype(BF16),
                              rw_prm[4], tp["fin_m"])
        merged = _merge_call(ycat, proj, w_branch[l].astype(BF16), tp["merge_m"], tp["merge_n"])
        last = l == depth - 1
        g_next = (final_g if last else norm_g[l + 1]).reshape(1, D_MODEL)
        res = _outproj_call(x, merged, w_out[l].astype(BF16), g_next, tp["out_m"], last)
        if last:
            return res
        x, h = res


def kernel(x_prompt, x_sample, norm_g, w_in, lru_conv_w, lru_conv_b, lru_w_r, lru_b_r, lru_w_i, lru_b_i, lru_lambda, rwkv_mu, rwkv_w0, rwkv_w2, rwkv_a0, rwkv_a2, rwkv_k_k, rwkv_k_a, rwkv_r_k, rwkv_lnx_g, rwkv_lnx_b, ret_gn_g, s5_lam_re, s5_lam_im, s5_log_step, s5_b_re, s5_b_im, s5_c_re, s5_c_im, s5_d, s5_glu_w, s5_glu_b, w_branch, w_out, final_g):
    bp, lp, d = x_prompt.shape
    bs, ls, _ = x_sample.shape
    seq_lens = (lp,) * bp + (ls,) * bs
    x = jnp.concatenate([x_prompt.reshape(bp * lp, d), x_sample.reshape(bs * ls, d)], axis=0)
    y = _trunk(x, seq_lens, norm_g, w_in, lru_conv_w, lru_conv_b, lru_w_r, lru_b_r, lru_w_i, lru_b_i, lru_lambda,
               rwkv_mu, rwkv_w0, rwkv_w2, rwkv_a0, rwkv_a2, rwkv_k_k, rwkv_k_a, rwkv_r_k, rwkv_lnx_g, rwkv_lnx_b,
               ret_gn_g, s5_lam_re, s5_lam_im, s5_log_step, s5_b_re, s5_b_im, s5_c_re, s5_c_im, s5_d, s5_glu_w,
               s5_glu_b, w_branch, w_out, final_g)
    return y[:bp * lp].reshape(bp, lp, d), y[bp * lp:].reshape(bs, ls, d)
```

```python
import functools
import math

import numpy as np
import jax
import jax.numpy as jnp
from jax import lax
from jax.experimental import pallas as pl
from jax.experimental.pallas import tpu as pltpu

F32 = jnp.float32
BF16 = jnp.bfloat16
SWEEP_OUT = BF16

D_MODEL = 2048
DEPTH = 4
W_BR = D_MODEL // 2
N_BRANCH = 4
LRU_BLOCKS = 8
LRU_BS = W_BR // LRU_BLOCKS
LRU_C = 8.0
RWKV_HEAD = 64
RWKV_H = W_BR // RWKV_HEAD
RWKV_LORA = 64
RWKV_LN_EPS = RWKV_HEAD * 1e-5
RWKV_CHUNK = 64
RET_H = 4
RET_DK = W_BR // 2 // RET_H
RET_DV = W_BR // RET_H
RET_CHUNK = 128
RET_GN_EPS = 1e-5
ROPE_BASE = 10000.0
S5_P = 16
S5_G = W_BR // S5_P
S5_N = 64
S5_GB = 8
S5_NB = S5_G // S5_GB
S5_SW = S5_GB * S5_N
LAM_RE_MAX = -1e-4
NORM_EPS = 1e-6

SUBLANES = 8
LANES = 128
VMEM_LIMIT = 56 * 1024 * 1024

_ORIG_SPLITS = (
    ("lru_x", W_BR), ("lru_z", W_BR),
    ("rw_r", W_BR), ("rw_k", W_BR), ("rw_v", W_BR), ("rw_wdf", RWKV_LORA), ("rw_wdb", RWKV_LORA),
    ("rw_ad", RWKV_LORA), ("rw_z", W_BR),
    ("ret_q", RET_H * RET_DK), ("ret_k", RET_H * RET_DK), ("ret_v", W_BR), ("ret_z", W_BR),
    ("s5_u", W_BR), ("s5_z", W_BR),
    ("gates", N_BRANCH * D_MODEL),
)
_NEW_ORDER = ("lru_x", "lru_z", "rw_r", "rw_k", "rw_v", "rw_z", "ret_v", "ret_z", "s5_u", "s5_z", "gates",
              "ret_q", "ret_k", "rw_wdf", "rw_wdb", "rw_ad")


def _column_layout():
    orig, start = {}, 0
    for name, n in _ORIG_SPLITS:
        orig[name] = (start, n)
        start += n
    new, perm, pos = {}, [], 0
    for name in _NEW_ORDER:
        s, n = orig[name]
        new[name] = pos
        perm.extend(range(s, s + n))
        pos += n
    pad = (-pos) % 256
    return new, np.asarray(perm, np.int32), pos, pad


COL, _PERM, _N_IN, _N_PAD = _column_layout()
N_COLS = _N_IN + _N_PAD


def _cparams(sem):
    return pltpu.CompilerParams(dimension_semantics=sem, vmem_limit_bytes=VMEM_LIMIT)


def _sigmoid(x):
    return 1.0 / (1.0 + jnp.exp(-x))


def _softplus(x):
    return jnp.maximum(x, 0.0) + jnp.log1p(jnp.exp(-jnp.abs(x)))


def _one_minus_exp(x):
    series = -x * (1.0 + x * (0.5 + x * (1.0 / 6.0 + x * (1.0 / 24.0))))
    return jnp.where(x > -0.03, series, 1.0 - jnp.exp(x))


def _silu(x):
    return x * _sigmoid(x)


def _bdot(a, b):
    return jnp.dot(a.astype(BF16), b.astype(BF16), preferred_element_type=F32)


def _split_dot(x, w):
    hi = x.astype(BF16)
    lo = (x - hi.astype(F32)).astype(BF16)
    return (jnp.dot(hi, w, preferred_element_type=F32) + jnp.dot(lo, w, preferred_element_type=F32))


def _rmsnorm_rows(x, g):
    return x * lax.rsqrt(jnp.mean(x * x, axis=-1, keepdims=True) + NORM_EPS) * g


def _norm_call(x, g, tm):
    n, d = x.shape

    def body(x_ref, g_ref, o_ref):
        o_ref[...] = _rmsnorm_rows(x_ref[...], g_ref[...]).astype(BF16)

    return pl.pallas_call(
        body, grid=(n // tm,),
        in_specs=[pl.BlockSpec((tm, d), lambda i: (i, 0)), pl.BlockSpec((1, d), lambda i: (0, 0))],
        out_specs=pl.BlockSpec((tm, d), lambda i: (i, 0)),
        out_shape=jax.ShapeDtypeStruct((n, d), BF16),
        compiler_params=_cparams(("parallel",)), name="rmsnorm")(x, g)


def _inproj_call(h, w, tm, tn):
    n, k = h.shape
    nc = w.shape[1]

    def body(h_ref, w_ref, o_ref):
        o_ref[...] = jnp.dot(h_ref[...], w_ref[...], preferred_element_type=F32)

    return pl.pallas_call(
        body, grid=(nc // tn, n // tm),
        in_specs=[pl.BlockSpec((tm, k), lambda j, i: (i, 0)), pl.BlockSpec((k, tn), lambda j, i: (0, j))],
        out_specs=pl.BlockSpec((tm, tn), lambda j, i: (i, j)),
        out_shape=jax.ShapeDtypeStruct((n, nc), F32),
        compiler_params=_cparams(("parallel", "arbitrary")), name="inproj")(h, w)


def _outproj_call(x, merged, w_out, g, tm, last):
    n, d = x.shape

    def body(x_ref, m_ref, w_ref, g_ref, *o_refs):
        y = x_ref[...] + jnp.dot(m_ref[...], w_ref[...], preferred_element_type=F32)
        hn = _rmsnorm_rows(y, g_ref[...])
        if last:
            o_refs[0][...] = hn
        else:
            o_refs[0][...] = y
            o_refs[1][...] = hn.astype(BF16)

    row = pl.BlockSpec((tm, d), lambda i: (i, 0))
    if last:
        out_specs, out_shape = row, jax.ShapeDtypeStruct((n, d), F32)
    else:
        out_specs = (row, row)
        out_shape = (jax.ShapeDtypeStruct((n, d), F32), jax.ShapeDtypeStruct((n, d), BF16))
    return pl.pallas_call(
        body, grid=(n // tm,),
        in_specs=[row, row, pl.BlockSpec((d, d), lambda i: (0, 0)), pl.BlockSpec((1, d), lambda i: (0, 0))],
        out_specs=out_specs, out_shape=out_shape,
        compiler_params=_cparams(("parallel",)), name="outproj")(x, merged, w_out, g)


def _gelu_tanh(x):
    return 0.5 * x * (1.0 + jnp.tanh(math.sqrt(2.0 / math.pi) * (x + 0.044715 * (x * x * x))))


def _finalize_call(proj, lru_h, rw_y, rw_bonus, ret_o, s5_y, fprm, glu_w, ones_bd, tm):
    n = proj.shape[0]
    w = W_BR

    def body(lz_ref, rz_ref, cz_ref, su_ref, sz_ref, lh_ref, ry_ref, rb_ref, co_ref, sy_ref, p_ref, glu_ref,
             ones_ref, o_ref):
        def both(ref, sl=slice(None)):
            return ref[0, :, sl].astype(F32) + ref[1, :, sl].astype(F32)

        o_ref[0] = (both(lh_ref) * _silu(lz_ref[...])).astype(BF16)

        y = both(ry_ref)
        ones = ones_ref[...]
        yc = y - _split_dot(y, ones) * (1.0 / RWKV_HEAD)
        var = _split_dot(yc * yc, ones) * (1.0 / RWKV_HEAD)
        yb = yc * lax.rsqrt(var + RWKV_LN_EPS) * p_ref[0:1, :] + p_ref[1:2, :] + rb_ref[...].astype(F32)
        o_ref[1] = (yb * _silu(rz_ref[...])).astype(BF16)

        for h in range(RET_H):
            sl = slice(RET_DV * h, RET_DV * (h + 1))
            oh = both(co_ref, sl)
            oc = oh - jnp.mean(oh, axis=-1, keepdims=True)
            ov = jnp.mean(oc * oc, axis=-1, keepdims=True)
            yc_h = oc * lax.rsqrt(ov + RET_GN_EPS) * p_ref[2:3, sl]
            o_ref[2, :, sl] = (yc_h * _silu(cz_ref[:, sl])).astype(BF16)

        s = _gelu_tanh(both(sy_ref) + p_ref[3:4, :] * su_ref[...])
        s = s * _sigmoid(_bdot(s, glu_ref[...]) + p_ref[4:5, :])
        o_ref[3] = (s * _silu(sz_ref[...])).astype(BF16)

    def col(name):
        cb = COL[name] // w
        return pl.BlockSpec((tm, w), lambda i: (i, cb))

    both = pl.BlockSpec((2, tm, w), lambda i: (0, i, 0))
    first = pl.BlockSpec((None, tm, w), lambda i: (0, i, 0))

    def const(shape):
        return pl.BlockSpec(tuple(shape), lambda i: (0,) * len(shape))

    return pl.pallas_call(
        body, grid=(n // tm,),
        in_specs=[col("lru_z"), col("rw_z"), col("ret_z"), col("s5_u"), col("s5_z"), both, both, first, both, both,
                  const(fprm.shape), const(glu_w.shape), const(ones_bd.shape)],
        out_specs=pl.BlockSpec((N_BRANCH, tm, w), lambda i: (0, i, 0)),
        out_shape=jax.ShapeDtypeStruct((N_BRANCH, n, w), BF16),
        compiler_params=_cparams(("parallel",)), name="finalize")(
            proj, proj, proj, proj, proj, lru_h, rw_y, rw_bonus, ret_o, s5_y, fprm, glu_w, ones_bd)


def _merge_call(ycat, proj, w_branch, tm, tn):
    n = proj.shape[0]
    gate0 = COL["gates"] // tn
    per_branch = D_MODEL // tn

    def body(y_ref, g0, g1, g2, g3, w_ref, o_ref):
        acc = None
        for b, g_ref in enumerate((g0, g1, g2, g3)):
            term = _sigmoid(g_ref[...]) * jnp.dot(y_ref[b], w_ref[b], preferred_element_type=F32)
            acc = term if acc is None else acc + term
        o_ref[...] = acc.astype(BF16)

    gates = [pl.BlockSpec((tm, tn), lambda j, i, b=b: (i, gate0 + b * per_branch + j)) for b in range(N_BRANCH)]
    return pl.pallas_call(
        body, grid=(D_MODEL // tn, n // tm),
        in_specs=[pl.BlockSpec((N_BRANCH, tm, W_BR), lambda j, i: (0, i, 0)), *gates,
                  pl.BlockSpec((N_BRANCH, W_BR, tn), lambda j, i: (0, 0, j))],
        out_specs=pl.BlockSpec((tm, tn), lambda j, i: (i, j)),
        out_shape=jax.ShapeDtypeStruct((n, D_MODEL), BF16),
        compiler_params=_cparams(("parallel", "arbitrary")), name="merge")(
            ycat, proj, proj, proj, proj, w_branch)


def _seq_flags(seq_lens, t):
    sf, ef = [], []
    for ln in seq_lens:
        assert ln % t == 0
        k = ln // t
        sf += [1] + [0] * (k - 1)
        ef += [0] * (k - 1) + [1]
    return jnp.asarray(sf, jnp.int32), jnp.asarray(ef, jnp.int32)


def _tile_of(d, i, n_t):
    return i + d * (n_t - 1 - 2 * i)


def _tile_spec(t, w, col, n_t):
    cb = col // w
    return pl.BlockSpec((t, w), lambda d, i, sf, ef: (_tile_of(d, i, n_t), cb))


def _halo_specs(t, w, col, n_t):
    cb = col // w
    hb = t // SUBLANES
    last = n_t * hb - 1
    prev = pl.BlockSpec((SUBLANES, w), lambda d, i, sf, ef: (jnp.maximum(_tile_of(d, i, n_t) * hb - 1, 0), cb))
    nxt = pl.BlockSpec((SUBLANES, w), lambda d, i, sf, ef: (jnp.minimum((_tile_of(d, i, n_t) + 1) * hb, last), cb))
    return prev, nxt


def _dir_spec(shape):
    nd = len(shape)
    return pl.BlockSpec((None,) + tuple(shape[1:]), lambda d, i, sf, ef: (d,) + (0,) * (nd - 1))


def _const_spec(shape):
    nd = len(shape)
    return pl.BlockSpec(tuple(shape), lambda d, i, sf, ef: (0,) * nd)


def _fill_halo(buf, x_ref, xp_ref, xn_ref, keep_p, keep_n, t):
    buf[0:SUBLANES, :] = xp_ref[...] * keep_p
    buf[SUBLANES:t + SUBLANES, :] = x_ref[...]
    buf[t + SUBLANES:t + 2 * SUBLANES, :] = xn_ref[...] * keep_n


def _seg_pitch(t):
    return t // SUBLANES + 4


def _seg_rows(t):
    return SUBLANES * _seg_pitch(t)


def _to_segments(ref, blk, val, t):
    s_len, pitch = t // SUBLANES, _seg_pitch(t)
    for k in range(SUBLANES):
        ref[blk, pitch * k:pitch * k + s_len, :] = val[s_len * k:s_len * (k + 1), :]


def _from_segments(ref, blk, t):
    s_len, pitch = t // SUBLANES, _seg_pitch(t)
    return jnp.concatenate([ref[blk, pitch * k:pitch * k + s_len, :] for k in range(SUBLANES)], axis=0)


def _seg_scan_real(a_ref, b_ref, o_ref, carry_ref, t, rev):
    s_len = t // SUBLANES
    pitch = _seg_pitch(t)
    nb = a_ref.shape[0]
    blocks = range(nb)

    def ld(ref, cb, s):
        return ref[cb, pl.ds(s, SUBLANES, stride=pitch), :]

    def step_of(j):
        return s_len - 1 - j if rev else j

    def pass1(j, hp):
        hs, ps = hp
        s = step_of(j)
        a = [ld(a_ref, cb, s) for cb in blocks]
        return (tuple(a[cb] * hs[cb] + ld(b_ref, cb, s) for cb in blocks),
                tuple(a[cb] * ps[cb] for cb in blocks))

    zero = jnp.zeros((SUBLANES, LANES), F32)
    es, ps = lax.fori_loop(0, s_len, pass1, ((zero,) * nb, (zero + 1.0,) * nb), unroll=2)
    starts = []
    for cb in blocks:
        c = carry_ref[cb, 0:1, :]
        rows = [None] * SUBLANES
        for k in (range(SUBLANES - 1, -1, -1) if rev else range(SUBLANES)):
            rows[k] = c
            c = ps[cb][k:k + 1, :] * c + es[cb][k:k + 1, :]
        carry_ref[cb, 0:1, :] = c
        starts.append(jnp.concatenate(rows, axis=0))

    def pass2(j, hs):
        s = step_of(j)
        out = []
        for cb in blocks:
            h = ld(a_ref, cb, s) * hs[cb] + ld(b_ref, cb, s)
            o_ref[cb, pl.ds(s, SUBLANES, stride=pitch), :] = h
            out.append(h)
        return tuple(out)

    lax.fori_loop(0, s_len, pass2, tuple(starts), unroll=2)


def _lru_call(proj, conv_w, conv_b, wcat, bcat, sp, flags, t):
    n = proj.shape[0]
    n_t = n // t
    w = W_BR

    def body(sf_ref, ef_ref, x_ref, xp_ref, xn_ref, cw_ref, cb_ref, w_ref, b_ref, sp_ref, o_ref,
             xbuf, a_scr, b_scr, h_scr, carry):
        d = pl.program_id(0)
        ti = _tile_of(d, pl.program_id(1), n_t)
        sf = sf_ref[ti]
        ef = ef_ref[ti]
        _fill_halo(xbuf, x_ref, xp_ref, xn_ref, (1 - sf).astype(F32), (1 - ef).astype(F32), t)
        o = SUBLANES
        xc = (cw_ref[0:1, :] * xbuf[o - 2:o - 2 + t, :] + cw_ref[1:2, :] * xbuf[o - 1:o - 1 + t, :]
              + cw_ref[2:3, :] * xbuf[o:o + t, :] + cw_ref[3:4, :] * xbuf[o + 1:o + 1 + t, :] + cb_ref[...])
        row = lax.broadcasted_iota(jnp.int32, (t, LRU_BS), 0)
        first_row = jnp.where(d == 0, 0, t - 1)
        at_seq_edge = jnp.where(d == 0, sf, ef) == 1
        first = jnp.logical_and(row == first_row, at_seq_edge)
        for hb in range(LRU_BLOCKS):
            sl = slice(LRU_BS * hb, LRU_BS * (hb + 1))
            xcb = xc[:, sl]
            g = _bdot(xcb, w_ref[hb])
            r = _sigmoid(g[:, :LRU_BS] + b_ref[0:1, sl])
            ig = _sigmoid(g[:, LRU_BS:] + b_ref[1:2, sl])
            log_a = (-LRU_C) * r * _softplus(-sp_ref[0:1, sl])
            mult = jnp.sqrt(_one_minus_exp(2.0 * log_a))
            mult = jnp.where(first, 1.0, mult)
            _to_segments(a_scr, hb, jnp.exp(log_a), t)
            _to_segments(b_scr, hb, mult * ig * xcb, t)

        @pl.when(at_seq_edge)
        def _():
            carry[...] = jnp.zeros_like(carry)

        @pl.when(d == 0)
        def _():
            _seg_scan_real(a_scr, b_scr, h_scr, carry, t, False)

        @pl.when(d == 1)
        def _():
            _seg_scan_real(a_scr, b_scr, h_scr, carry, t, True)

        for hb in range(LRU_BLOCKS):
            o_ref[:, LRU_BS * hb:LRU_BS * (hb + 1)] = _from_segments(h_scr, hb, t).astype(SWEEP_OUT)

    prev, nxt = _halo_specs(t, w, COL["lru_x"], n_t)
    grid_spec = pltpu.PrefetchScalarGridSpec(
        num_scalar_prefetch=2, grid=(2, n_t),
        in_specs=[_tile_spec(t, w, COL["lru_x"], n_t), prev, nxt,
                  _const_spec(conv_w.shape), _const_spec(conv_b.shape),
                  _dir_spec(wcat.shape), _dir_spec(bcat.shape), _dir_spec(sp.shape)],
        out_specs=pl.BlockSpec((None, t, w), lambda d, i, sf, ef: (d, _tile_of(d, i, n_t), 0)),
        scratch_shapes=[pltpu.VMEM((t + 2 * SUBLANES, w), F32)]
        + [pltpu.VMEM((LRU_BLOCKS, _seg_rows(t), LRU_BS), F32)] * 3
        + [pltpu.VMEM((LRU_BLOCKS, SUBLANES, LRU_BS), F32)])
    return pl.pallas_call(
        body, grid_spec=grid_spec, out_shape=jax.ShapeDtypeStruct((2, n, w), SWEEP_OUT),
        compiler_params=_cparams(("arbitrary", "arbitrary")), name="lru_sweep")(
            *flags, proj, proj, proj, conv_w, conv_b, wcat, bcat, sp)


def _prep_lru(conv_w, conv_b, w_r, b_r, w_i, b_i, lam):
    wcat = jnp.concatenate([w_r, w_i], axis=-1).astype(BF16)
    bcat = jnp.stack([b_r, b_i], axis=1)
    return conv_w, conv_b.reshape(1, W_BR), wcat, bcat, lam.reshape(2, 1, W_BR)


_NT3 = (((2,), (2,)), ((0,), (0,)))
_NN3 = (((2,), (1,)), ((0,), (0,)))


def _nt(a, b):
    return lax.dot_general(a.astype(BF16), b.astype(BF16), _NT3, preferred_element_type=F32)


def _nn(a, b):
    return lax.dot_general(a.astype(BF16), b.astype(BF16), _NN3, preferred_element_type=F32)


def _rwkv_call(proj, prm, w0, w2, a2, ones_bd, tri, flags, t):
    n = proj.shape[0]
    n_t = n // t
    w = W_BR
    c = RWKV_CHUNK
    n_c = t // c
    nh, hd = RWKV_H, RWKV_HEAD

    def body(sf_ref, ef_ref, r_ref, rp_ref, rn_ref, k_ref, kp_ref, kn_ref, v_ref, vp_ref, vn_ref, lora_ref,
             prm_ref, w0_ref, w2_ref, a2_ref, ones_ref, tri_ref, y_ref, bonus_ref,
             xbuf, art_s, bk_s, vt_s, top_s, bot_s, tinv_s, yt_s, gam_s, state):
        d = pl.program_id(0)
        ti = _tile_of(d, pl.program_id(1), n_t)
        sf = sf_ref[ti]
        ef = ef_ref[ti]
        keep_p, keep_n = (1 - sf).astype(F32), (1 - ef).astype(F32)
        o = SUBLANES

        def mixed(x_ref, xp_ref, xn_ref, mu):
            _fill_halo(xbuf, x_ref, xp_ref, xn_ref, keep_p, keep_n, t)
            x = xbuf[o:o + t, :]
            return x + mu * (0.5 * (xbuf[o - 1:o - 1 + t, :] + xbuf[o + 1:o + 1 + t, :]) - x)

        r = mixed(r_ref, rp_ref, rn_ref, prm_ref[0:1, :])
        k = mixed(k_ref, kp_ref, kn_ref, prm_ref[1:2, :])
        v = mixed(v_ref, vp_ref, vn_ref, prm_ref[2:3, :])
        lora = lora_ref[...]
        wd = jnp.where(d == 0, lora[:, 0:RWKV_LORA], lora[:, RWKV_LORA:2 * RWKV_LORA])
        ad = lora[:, 2 * RWKV_LORA:3 * RWKV_LORA]
        a_icl = _sigmoid(prm_ref[3:4, :] + _bdot(ad, a2_ref[...]))
        w_log = -_softplus(-(w0_ref[...] + _bdot(jnp.tanh(wd), w2_ref[...]))) - 0.5
        logw = -jnp.exp(w_log)
        kkr = k * prm_ref[4:5, :]
        kk = kkr / jnp.maximum(jnp.sqrt(_bdot(kkr * kkr, ones_ref[...])), 1e-12)
        k_mod = k * (1.0 + (a_icl - 1.0) * prm_ref[5:6, :])
        bonus_ref[...] = (_split_dot(r * k_mod * prm_ref[6:7, :], ones_ref[...]) * v).astype(SWEEP_OUT)
        b_vec = kk * a_icl

        ii = lax.broadcasted_iota(jnp.int32, (c, 2 * c), 0)
        jj = lax.broadcasted_iota(jnp.int32, (c, 2 * c), 1)
        jj = jnp.where(jj >= c, jj - c, jj)
        order = (1 - 2 * d) * (ii - jj)
        strict = (order > 0)[None]
        incl = (order >= 0)[None]
        eye = (lax.broadcasted_iota(jnp.int32, (c, c), 0)
               == lax.broadcasted_iota(jnp.int32, (c, c), 1)).astype(F32)[None]

        vt3 = v.T.reshape(nh, hd, t)
        tri_m = tri_ref[...]
        for cc in range(n_c):
            rows = slice(c * cc, c * (cc + 1))
            lw = logw[rows, :]
            hi = lw.astype(BF16)
            lo = (lw - hi.astype(F32)).astype(BF16)
            g = jnp.dot(tri_m, hi, preferred_element_type=F32) + jnp.dot(tri_m, lo, preferred_element_type=F32)
            e_pos = jnp.exp(g)
            e_neg = jnp.exp(-g)
            at_c = -kk[rows, :] * jnp.exp(g - lw)
            rt_c = r[rows, :] * e_pos
            bt_c = b_vec[rows, :] * e_neg
            kt_c = k_mod[rows, :] * e_neg
            g_end = jnp.exp(jnp.where(d == 0, g[c - 1:c, :], g[0:1, :]))
            for h in range(nh):
                ls = slice(hd * h, hd * (h + 1))
                art_s[cc, h, 0:c, :] = at_c[:, ls]
                art_s[cc, h, c:2 * c, :] = rt_c[:, ls]
                bk_s[cc, h, 0:c, :] = bt_c[:, ls]
                bk_s[cc, h, c:2 * c, :] = kt_c[:, ls]
                gam_s[h, SUBLANES * cc:SUBLANES * (cc + 1), :] = jnp.broadcast_to(g_end[:, ls], (SUBLANES, hd))
            vt_c = vt3[:, :, rows]
            vt_s[cc] = jnp.concatenate([jnp.zeros_like(vt_c), vt_c], axis=2)

        @pl.when(jnp.where(d == 0, sf, ef) == 1)
        def _():
            state[...] = jnp.zeros_like(state)

        def chunk_of(j):
            return jnp.where(d == 0, j, n_c - 1 - j)

        def prepare(p, carry):
            pair = pl.ds(pl.multiple_of(2 * p, 2), 2)
            a_all = _nt(art_s[pair].reshape(2 * nh, 2 * c, hd), bk_s[pair].reshape(2 * nh, 2 * c, hd))
            top = jnp.where(strict, a_all[:, :c], 0.0)
            a_ab = top[:, :, :c]
            tinv = eye + a_ab
            pw = a_ab
            for _ in range(int(math.log2(c)) - 1):
                pw = _nn(pw, pw)
                tinv = tinv + _nn(pw, tinv)
            top_s[pair] = top.reshape(2, nh, c, 2 * c)
            bot_s[pair] = jnp.where(incl, a_all[:, c:], 0.0).reshape(2, nh, c, 2 * c)
            tinv_s[pair] = tinv.reshape(2, nh, c, c)
            return carry

        def advance(j, carry):
            cc = chunk_of(j)
            art = art_s[cc]
            bk = bk_s[cc]
            at, rt = art[:, :c], art[:, c:]
            vt_hi = vt_s[cc]
            gam = gam_s[:, pl.ds(pl.multiple_of(cc * SUBLANES, SUBLANES), 1), :]
            s0 = state[...]
            zt = _nt(s0, at) + _nt(vt_hi, top_s[cc])
            ut = _nt(zt, tinv_s[cc])
            ut_vt = jnp.concatenate([ut, jnp.zeros_like(ut)], axis=2) + vt_hi
            yt_s[cc] = _nt(s0, rt) + _nt(ut_vt, bot_s[cc])
            state[...] = (s0 + _nn(ut_vt, bk)) * gam
            return carry

        lax.fori_loop(0, n_c // 2, prepare, 0)
        lax.fori_loop(0, n_c, advance, 0)
        yt = jnp.concatenate([yt_s[cc] for cc in range(n_c)], axis=2)
        y_ref[...] = yt.reshape(w, t).T.astype(SWEEP_OUT)

    lora_col = COL["rw_wdf"]
    r_halo = _halo_specs(t, w, COL["rw_r"], n_t)
    k_halo = _halo_specs(t, w, COL["rw_k"], n_t)
    v_halo = _halo_specs(t, w, COL["rw_v"], n_t)
    stacked = pltpu.VMEM((n_c, nh, 2 * c, hd), F32)
    paired = pltpu.VMEM((n_c, nh, hd, 2 * c), F32)
    grid_spec = pltpu.PrefetchScalarGridSpec(
        num_scalar_prefetch=2, grid=(2, n_t),
        in_specs=[_tile_spec(t, w, COL["rw_r"], n_t), *r_halo, _tile_spec(t, w, COL["rw_k"], n_t), *k_halo,
                  _tile_spec(t, w, COL["rw_v"], n_t), *v_halo, _tile_spec(t, 4 * RWKV_LORA, lora_col, n_t),
                  _const_spec(prm.shape), _dir_spec(w0.shape), _dir_spec(w2.shape), _const_spec(a2.shape),
                  _const_spec(ones_bd.shape), _dir_spec(tri.shape)],
        out_specs=(pl.BlockSpec((None, t, w), lambda d, i, sf, ef: (d, _tile_of(d, i, n_t), 0)),
                   pl.BlockSpec((None, t, w), lambda d, i, sf, ef: (d, _tile_of(d, i, n_t), 0))),
        scratch_shapes=[pltpu.VMEM((t + 2 * SUBLANES, w), F32), stacked, stacked,
                        paired, paired, paired, pltpu.VMEM((n_c, nh, c, c), F32),
                        pltpu.VMEM((n_c, nh, hd, c), F32),
                        pltpu.VMEM((nh, n_c * SUBLANES, hd), F32),
                        pltpu.VMEM((nh, hd, hd), F32)])
    return pl.pallas_call(
        body, grid_spec=grid_spec,
        out_shape=(jax.ShapeDtypeStruct((2, n, w), SWEEP_OUT), jax.ShapeDtypeStruct((2, n, w), SWEEP_OUT)),
        compiler_params=_cparams(("arbitrary", "arbitrary")), name="rwkv_sweep")(
            *flags, proj, proj, proj, proj, proj, proj, proj, proj, proj, proj, prm, w0, w2, a2, ones_bd, tri)


def _prep_rwkv(mu, w0, w2, a0, a2, k_k, k_a, r_k):
    prm = jnp.concatenate([mu, a0[None], k_k[None], k_a[None], r_k.reshape(1, W_BR),
                           jnp.zeros((1, W_BR), F32)], axis=0)
    head = np.arange(W_BR) // RWKV_HEAD
    ones_bd = jnp.asarray(head[:, None] == head[None, :], BF16)
    i = np.arange(RWKV_CHUNK)
    tri = jnp.asarray(np.stack([i[:, None] >= i[None, :], i[:, None] <= i[None, :]]), BF16)
    return prm, w0.reshape(2, 1, W_BR), w2.astype(BF16), a2.astype(BF16), ones_bd, tri


def _head_norm_ref(y, eps, hd):
    yh = y.reshape(y.shape[0], -1, hd)
    mean = jnp.mean(yh, axis=-1, keepdims=True)
    yc = yh - mean
    var = jnp.mean(yc * yc, axis=-1, keepdims=True)
    return (yc * lax.rsqrt(var + eps)).reshape(y.shape)


def _ret_tables(seq_lens):
    half = RET_DK // 2
    pos = jnp.concatenate([jnp.arange(ln, dtype=F32) for ln in seq_lens])
    inv = ROPE_BASE ** (-jnp.arange(half, dtype=F32) / half)
    ang = pos[:, None] * inv[None, :]
    cos, sin = jnp.cos(ang), jnp.sin(ang)
    cos_t = jnp.concatenate([cos, cos], axis=1)
    sin_t = jnp.concatenate([-sin, sin], axis=1)
    c = RET_CHUNK
    log_g = np.log(1.0 - 2.0 ** (-5.0 - np.arange(RET_H, dtype=np.float64)))
    i = np.arange(c, dtype=np.float64)
    dmat = np.exp(log_g[:, None, None] * np.abs(i[:, None] - i[None, :]))
    rows = np.stack([np.exp(log_g[:, None] * (i + 1.0)), np.exp(log_g[:, None] * (c - i)),
                     np.exp(log_g[:, None] * (c - 1.0 - i)), np.exp(log_g[:, None] * i)], axis=1)
    rows = np.broadcast_to(rows[..., None], (RET_H, 4, c, RET_DK))
    return cos_t, sin_t, jnp.asarray(dmat, F32), jnp.asarray(rows, F32)


def _ret_call(proj, cos_t, sin_t, dmat, rowsc, flags, t):
    n = proj.shape[0]
    n_t = n // t
    c = RET_CHUNK
    n_c = t // c
    g_chunk = [float((1.0 - 2.0 ** (-5.0 - h)) ** c) for h in range(RET_H)]
    tn_dims = (((0,), (0,)), ((), ()))
    nt_dims = (((1,), (1,)), ((), ()))

    def body(sf_ref, ef_ref, q_ref, k_ref, v_ref, cos_ref, sin_ref, dm_ref, rs_ref, o_ref, state):
        d = pl.program_id(0)
        ti = _tile_of(d, pl.program_id(1), n_t)
        at_seq_edge = jnp.where(d == 0, sf_ref[ti], ef_ref[ti]) == 1

        @pl.when(at_seq_edge)
        def _():
            state[...] = jnp.zeros_like(state)

        def rot(x_ref, rows, h):
            x = x_ref[rows, RET_DK * h:RET_DK * (h + 1)]
            return x * cos_ref[rows, :] + pltpu.roll(x, RET_DK // 2, 1) * sin_ref[rows, :]

        def chunk(cc, fwd):
            rows = slice(c * cc, c * (cc + 1))
            for h in range(RET_H):
                qh = rot(q_ref, rows, h)
                kh = rot(k_ref, rows, h) * (RET_DK ** -0.5)
                vh = v_ref[rows, RET_DV * h:RET_DV * (h + 1)].astype(BF16)
                s_old = state[h]
                if fwd:
                    sc = lax.dot_general(qh.astype(BF16), kh.astype(BF16), nt_dims,
                                         preferred_element_type=F32) * dm_ref[h]
                    o = (jnp.dot(sc.astype(BF16), vh, preferred_element_type=F32)
                         + _bdot(qh * rs_ref[h, 0], s_old))
                    kd = kh * rs_ref[h, 2]
                else:
                    o = _bdot(qh * rs_ref[h, 1], s_old)
                    kd = kh * rs_ref[h, 3]
                o_ref[rows, RET_DV * h:RET_DV * (h + 1)] = o.astype(SWEEP_OUT)
                state[h] = g_chunk[h] * s_old + lax.dot_general(kd.astype(BF16), vh, tn_dims,
                                                                 preferred_element_type=F32)

        @pl.when(d == 0)
        def _():
            for cc in range(n_c):
                chunk(cc, True)

        @pl.when(d == 1)
        def _():
            for cc in range(n_c - 1, -1, -1):
                chunk(cc, False)

    hk = RET_H * RET_DK
    tab = pl.BlockSpec((t, RET_DK), lambda d, i, sf, ef: (_tile_of(d, i, n_t), 0))
    grid_spec = pltpu.PrefetchScalarGridSpec(
        num_scalar_prefetch=2, grid=(2, n_t),
        in_specs=[_tile_spec(t, hk, COL["ret_q"], n_t), _tile_spec(t, hk, COL["ret_k"], n_t),
                  _tile_spec(t, W_BR, COL["ret_v"], n_t), tab, tab,
                  _const_spec(dmat.shape), _const_spec(rowsc.shape)],
        out_specs=pl.BlockSpec((None, t, W_BR), lambda d, i, sf, ef: (d, _tile_of(d, i, n_t), 0)),
        scratch_shapes=[pltpu.VMEM((RET_H, RET_DK, RET_DV), F32)])
    return pl.pallas_call(
        body, grid_spec=grid_spec, out_shape=jax.ShapeDtypeStruct((2, n, W_BR), SWEEP_OUT),
        compiler_params=_cparams(("arbitrary", "arbitrary")), name="ret_sweep")(
            *flags, proj, proj, proj, cos_t, sin_t, dmat, rowsc)


def _seg_scan_complex(br_ref, bi_ref, sr_ref, si_ref, carry_r, carry_i, base, lam_ref, t, rev):
    s_len = t // SUBLANES
    pitch = _seg_pitch(t)
    nb = br_ref.shape[0]
    blocks = range(nb)
    lr = [jnp.broadcast_to(lam_ref[0:1, LANES * cb:LANES * (cb + 1)], (SUBLANES, LANES)) for cb in blocks]
    li = [jnp.broadcast_to(lam_ref[1:2, LANES * cb:LANES * (cb + 1)], (SUBLANES, LANES)) for cb in blocks]

    def ld(ref, cb, s):
        return ref[cb, pl.ds(s, SUBLANES, stride=pitch), :]

    def step_of(j):
        return s_len - 1 - j if rev else j

    def advance(cb, s, hr, hi):
        nr = lr[cb] * hr - li[cb] * hi + ld(br_ref, cb, s)
        ni = lr[cb] * hi + li[cb] * hr + ld(bi_ref, cb, s)
        return nr, ni

    def pass1(j, h):
        s = step_of(j)
        return tuple(advance(cb, s, *h[cb]) for cb in blocks)

    zero = jnp.zeros((SUBLANES, LANES), F32)
    ends = lax.fori_loop(0, s_len, pass1, ((zero, zero),) * nb, unroll=2)
    starts = []
    for cb in blocks:
        sl = slice(LANES * cb, LANES * (cb + 1))
        pr, pi = lam_ref[2:3, sl], lam_ref[3:4, sl]
        cr, ci = carry_r[base + cb, 0:1, :], carry_i[base + cb, 0:1, :]
        rows_r, rows_i = [None] * SUBLANES, [None] * SUBLANES
        for k in (range(SUBLANES - 1, -1, -1) if rev else range(SUBLANES)):
            rows_r[k], rows_i[k] = cr, ci
            er, ei = ends[cb][0][k:k + 1, :], ends[cb][1][k:k + 1, :]
            cr, ci = pr * cr - pi * ci + er, pr * ci + pi * cr + ei
        carry_r[base + cb, 0:1, :] = cr
        carry_i[base + cb, 0:1, :] = ci
        starts.append((jnp.concatenate(rows_r, axis=0), jnp.concatenate(rows_i, axis=0)))

    def pass2(j, h):
        s = step_of(j)
        out = []
        for cb in blocks:
            nr, ni = advance(cb, s, *h[cb])
            sr_ref[cb, pl.ds(s, SUBLANES, stride=pitch), :] = nr
            si_ref[cb, pl.ds(s, SUBLANES, stride=pitch), :] = ni
            out.append((nr, ni))
        return tuple(out)

    lax.fori_loop(0, s_len, pass2, tuple(starts), unroll=2)


def _s5_call(proj, bblk, cblk, lam4, flags, t):
    n = proj.shape[0]
    n_t = n // t
    w = W_BR
    lb = S5_SW // LANES

    def body(sf_ref, ef_ref, u_ref, b_ref, c_ref, lam_ref, o_ref, br, bi, sr, si, carry_r, carry_i):
        d = pl.program_id(0)
        ti = _tile_of(d, pl.program_id(1), n_t)
        at_seq_edge = jnp.where(d == 0, sf_ref[ti], ef_ref[ti]) == 1

        @pl.when(at_seq_edge)
        def _():
            carry_r[...] = jnp.zeros_like(carry_r)
            carry_i[...] = jnp.zeros_like(carry_i)

        for ob in range(S5_NB):
            bu = _bdot(u_ref[:, LANES * ob:LANES * (ob + 1)], b_ref[ob])
            for cb in range(lb):
                _to_segments(br, cb, bu[:, LANES * cb:LANES * (cb + 1)], t)
                _to_segments(bi, cb, bu[:, S5_SW + LANES * cb:S5_SW + LANES * (cb + 1)], t)

            @pl.when(d == 0)
            def _():
                _seg_scan_complex(br, bi, sr, si, carry_r, carry_i, ob * lb, lam_ref.at[ob], t, False)

            @pl.when(d == 1)
            def _():
                _seg_scan_complex(br, bi, sr, si, carry_r, carry_i, ob * lb, lam_ref.at[ob], t, True)

            st = jnp.concatenate([_from_segments(sr, cb, t) for cb in range(lb)]
                                 + [_from_segments(si, cb, t) for cb in range(lb)], axis=1)
            o_ref[:, LANES * ob:LANES * (ob + 1)] = _bdot(st, c_ref[ob]).astype(SWEEP_OUT)

    grid_spec = pltpu.PrefetchScalarGridSpec(
        num_scalar_prefetch=2, grid=(2, n_t),
        in_specs=[_tile_spec(t, w, COL["s5_u"], n_t), _const_spec(bblk.shape), _dir_spec(cblk.shape),
                  _dir_spec(lam4.shape)],
        out_specs=pl.BlockSpec((None, t, w), lambda d, i, sf, ef: (d, _tile_of(d, i, n_t), 0)),
        scratch_shapes=[pltpu.VMEM((lb, _seg_rows(t), LANES), F32)] * 4
        + [pltpu.VMEM((S5_NB * lb, SUBLANES, LANES), F32)] * 2)
    return pl.pallas_call(
        body, grid_spec=grid_spec, out_shape=jax.ShapeDtypeStruct((2, n, w), SWEEP_OUT),
        compiler_params=_cparams(("arbitrary", "arbitrary")), name="s5_sweep")(
            *flags, proj, bblk, cblk, lam4)


def _prep_s5(lam_re, lam_im, log_step, b_re, b_im, c_re, c_im, t):
    s_len = t // SUBLANES
    assert s_len & (s_len - 1) == 0
    lre = jnp.minimum(lam_re, LAM_RE_MAX)
    step = jnp.exp(log_step)[..., None]
    mag = jnp.exp(lre * step)
    lbr, lbi = mag * jnp.cos(lam_im * step), mag * jnp.sin(lam_im * step)
    den = lre * lre + lam_im * lam_im
    gr = ((lbr - 1.0) * lre + lbi * lam_im) / den
    gi = (lbi * lre - (lbr - 1.0) * lam_im) / den
    pr, pi = lbr, lbi
    for _ in range(int(math.log2(s_len))):
        pr, pi = pr * pr - pi * pi, 2.0 * pr * pi
    lam4 = jnp.stack([x.reshape(2, S5_NB, S5_SW) for x in (lbr, lbi, pr, pi)], axis=2)
    cpr = c_re[None] * gr[:, :, None, :] - c_im[None] * gi[:, :, None, :]
    cpi = c_re[None] * gi[:, :, None, :] + c_im[None] * gr[:, :, None, :]
    eye = jnp.eye(S5_GB, dtype=F32)

    def c_rows(x):
        x = x.reshape(2, S5_NB, S5_GB, S5_P, S5_N)
        return jnp.einsum('dogpn,gh->dognhp', x, eye).reshape(2, S5_NB, S5_SW, S5_GB * S5_P)

    cblk = jnp.concatenate([c_rows(cpr), c_rows(-cpi)], axis=2).astype(BF16)

    def b_cols(x):
        x = x.reshape(S5_NB, S5_GB, S5_N, S5_P)
        return jnp.einsum('ognp,gh->ogphn', x, eye).reshape(S5_NB, S5_GB * S5_P, S5_SW)

    bblk = jnp.concatenate([b_cols(b_re), b_cols(b_im)], axis=2).astype(BF16)
    return bblk, cblk, lam4


def _tile_plan(n, seq_lens):
    shortest = min(seq_lens)
    return dict(
        sweep=min(1024, shortest),
        rwkv=min(256, shortest),
        in_m=min(1024, n), in_n=1792,
        fin_m=min(256, n),
        merge_m=min(512, n), merge_n=1024,
        out_m=min(256, n),
        norm_m=min(512, n),
    )


def _permute_w_in(w_in):
    orig, start = {}, 0
    for name, nn in _ORIG_SPLITS:
        orig[name] = (start, nn)
        start += nn
    parts = [w_in[:, orig[name][0]:orig[name][0] + orig[name][1]] for name in _NEW_ORDER]
    parts.append(jnp.zeros((w_in.shape[0], _N_PAD), w_in.dtype))
    return jnp.concatenate(parts, axis=1).astype(BF16)


def _trunk(x, seq_lens, norm_g, w_in, lru_conv_w, lru_conv_b, lru_w_r, lru_b_r, lru_w_i, lru_b_i, lru_lambda,
           rwkv_mu, rwkv_w0, rwkv_w2, rwkv_a0, rwkv_a2, rwkv_k_k, rwkv_k_a, rwkv_r_k, rwkv_lnx_g, rwkv_lnx_b,
           ret_gn_g, s5_lam_re, s5_lam_im, s5_log_step, s5_b_re, s5_b_im, s5_c_re, s5_c_im, s5_d, s5_glu_w,
           s5_glu_b, w_branch, w_out, final_g):
    n = x.shape[0]
    tp = _tile_plan(n, seq_lens)
    flags = _seq_flags(seq_lens, tp["sweep"])
    flags_rw = _seq_flags(seq_lens, tp["rwkv"])
    ret_tabs = _ret_tables(seq_lens)
    depth = w_in.shape[0]
    h = _norm_call(x, norm_g[0].reshape(1, D_MODEL), tp["norm_m"])
    for l in range(depth):
        proj = _inproj_call(h, _permute_w_in(w_in[l]), tp["in_m"], tp["in_n"])
        lru_h = _lru_call(proj, *_prep_lru(lru_conv_w[l], lru_conv_b[l], lru_w_r[l], lru_b_r[l], lru_w_i[l],
                                           lru_b_i[l], lru_lambda[l]), flags, tp["sweep"])
        rw_prm = _prep_rwkv(rwkv_mu[l], rwkv_w0[l], rwkv_w2[l], rwkv_a0[l], rwkv_a2[l], rwkv_k_k[l], rwkv_k_a[l],
                            rwkv_r_k[l])
        rw_y, rw_bonus = _rwkv_call(proj, *rw_prm, flags_rw, tp["rwkv"])
        ret_o = _ret_call(proj, *ret_tabs, flags, tp["sweep"])
        s5_y = _s5_call(proj, *_prep_s5(s5_lam_re[l], s5_lam_im[l], s5_log_step[l], s5_b_re[l], s5_b_im[l],
                                        s5_c_re[l], s5_c_im[l], tp["sweep"]), flags, tp["sweep"])
        fprm = jnp.stack([rwkv_lnx_g[l], rwkv_lnx_b[l], ret_gn_g[l], s5_d[l], s5_glu_b[l],
                          jnp.zeros_like(s5_d[l]), jnp.zeros_like(s5_d[l]), jnp.zeros_like(s5_d[l])])
        ycat = _finalize_call(proj, lru_h, rw_y, rw_bonus, ret_o, s5_y, fprm, s5_glu_w[l].astype(BF16),
                              rw_prm[4], tp["fin_m"])
        merged = _merge_call(ycat, proj, w_branch[l].astype(BF16), tp["merge_m"], tp["merge_n"])
        last = l == depth - 1
        g_next = (final_g if last else norm_g[l + 1]).reshape(1, D_MODEL)
        res = _outproj_call(x, merged, w_out[l].astype(BF16), g_next, tp["out_m"], last)
        if last:
            return res
        x, h = res


def kernel(x_prompt, x_sample, norm_g, w_in, lru_conv_w, lru_conv_b, lru_w_r, lru_b_r, lru_w_i, lru_b_i, lru_lambda, rwkv_mu, rwkv_w0, rwkv_w2, rwkv_a0, rwkv_a2, rwkv_k_k, rwkv_k_a, rwkv_r_k, rwkv_lnx_g, rwkv_lnx_b, ret_gn_g, s5_lam_re, s5_lam_im, s5_log_step, s5_b_re, s5_b_im, s5_c_re, s5_c_im, s5_d, s5_glu_w, s5_glu_b, w_branch, w_out, final_g):
    bp, lp, d = x_prompt.shape
    bs, ls, _ = x_sample.shape
    seq_lens = (lp,) * bp + (ls,) * bs
    x = jnp.concatenate([x_prompt.reshape(bp * lp, d), x_sample.reshape(bs * ls, d)], axis=0)
    y = _trunk(x, seq_lens, norm_g, w_in, lru_conv_w, lru_conv_b, lru_w_r, lru_b_r, lru_w_i, lru_b_i, lru_lambda,
               rwkv_mu, rwkv_w0, rwkv_w2, rwkv_a0, rwkv_a2, rwkv_k_k, rwkv_k_a, rwkv_r_k, rwkv_lnx_g, rwkv_lnx_b,
               ret_gn_g, s5_lam_re, s5_lam_im, s5_log_step, s5_b_re, s5_b_im, s5_c_re, s5_c_im, s5_d, s5_glu_w,
               s5_glu_b, w_branch, w_out, final_g)
    return y[:bp * lp].reshape(bp, lp, d), y[bp * lp:].reshape(bs, ls, d)
```

```python
import functools
import math

import numpy as np
import jax
import jax.numpy as jnp
from jax import lax
from jax.experimental import pallas as pl
from jax.experimental.pallas import tpu as pltpu

F32 = jnp.float32
BF16 = jnp.bfloat16
SWEEP_OUT = BF16

D_MODEL = 2048
DEPTH = 4
W_BR = D_MODEL // 2
N_BRANCH = 4
LRU_BLOCKS = 8
LRU_BS = W_BR // LRU_BLOCKS
LRU_C = 8.0
RWKV_HEAD = 64
RWKV_H = W_BR // RWKV_HEAD
RWKV_LORA = 64
RWKV_LN_EPS = RWKV_HEAD * 1e-5
RWKV_CHUNK = 64
RET_H = 4
RET_DK = W_BR // 2 // RET_H
RET_DV = W_BR // RET_H
RET_CHUNK = 128
RET_GN_EPS = 1e-5
ROPE_BASE = 10000.0
S5_P = 16
S5_G = W_BR // S5_P
S5_N = 64
S5_GB = 8
S5_NB = S5_G // S5_GB
S5_SW = S5_GB * S5_N
LAM_RE_MAX = -1e-4
NORM_EPS = 1e-6

SUBLANES = 8
LANES = 128
VMEM_LIMIT = 56 * 1024 * 1024

_ORIG_SPLITS = (
    ("lru_x", W_BR), ("lru_z", W_BR),
    ("rw_r", W_BR), ("rw_k", W_BR), ("rw_v", W_BR), ("rw_wdf", RWKV_LORA), ("rw_wdb", RWKV_LORA),
    ("rw_ad", RWKV_LORA), ("rw_z", W_BR),
    ("ret_q", RET_H * RET_DK), ("ret_k", RET_H * RET_DK), ("ret_v", W_BR), ("ret_z", W_BR),
    ("s5_u", W_BR), ("s5_z", W_BR),
    ("gates", N_BRANCH * D_MODEL),
)
_NEW_ORDER = ("lru_x", "lru_z", "rw_r", "rw_k", "rw_v", "rw_z", "ret_v", "ret_z", "s5_u", "s5_z", "gates",
              "ret_q", "ret_k", "rw_wdf", "rw_wdb", "rw_ad")


def _column_layout():
    orig, start = {}, 0
    for name, n in _ORIG_SPLITS:
        orig[name] = (start, n)
        start += n
    new, perm, pos = {}, [], 0
    for name in _NEW_ORDER:
        s, n = orig[name]
        new[name] = pos
        perm.extend(range(s, s + n))
        pos += n
    pad = (-pos) % 256
    return new, np.asarray(perm, np.int32), pos, pad


COL, _PERM, _N_IN, _N_PAD = _column_layout()
N_COLS = _N_IN + _N_PAD


def _cparams(sem):
    return pltpu.CompilerParams(dimension_semantics=sem, vmem_limit_bytes=VMEM_LIMIT)


def _sigmoid(x):
    return 1.0 / (1.0 + jnp.exp(-x))


def _softplus(x):
    return jnp.maximum(x, 0.0) + jnp.log1p(jnp.exp(-jnp.abs(x)))


def _one_minus_exp(x):
    series = -x * (1.0 + x * (0.5 + x * (1.0 / 6.0 + x * (1.0 / 24.0))))
    return jnp.where(x > -0.03, series, 1.0 - jnp.exp(x))


def _silu(x):
    return x * _sigmoid(x)


def _bdot(a, b):
    return jnp.dot(a.astype(BF16), b.astype(BF16), preferred_element_type=F32)


def _split_dot(x, w):
    hi = x.astype(BF16)
    lo = (x - hi.astype(F32)).astype(BF16)
    return (jnp.dot(hi, w, preferred_element_type=F32) + jnp.dot(lo, w, preferred_element_type=F32))


def _rmsnorm_rows(x, g):
    return x * lax.rsqrt(jnp.mean(x * x, axis=-1, keepdims=True) + NORM_EPS) * g


def _norm_call(x, g, tm):
    n, d = x.shape

    def body(x_ref, g_ref, o_ref):
        o_ref[...] = _rmsnorm_rows(x_ref[...], g_ref[...]).astype(BF16)

    return pl.pallas_call(
        body, grid=(n // tm,),
        in_specs=[pl.BlockSpec((tm, d), lambda i: (i, 0)), pl.BlockSpec((1, d), lambda i: (0, 0))],
        out_specs=pl.BlockSpec((tm, d), lambda i: (i, 0)),
        out_shape=jax.ShapeDtypeStruct((n, d), BF16),
        compiler_params=_cparams(("parallel",)), name="rmsnorm")(x, g)


def _norm_concat_call(xa, xb, g, tm):
    na, d = xa.shape
    nb = xb.shape[0]
    ta = na // tm

    def body(xa_ref, xb_ref, g_ref, h_ref, x_ref):
        i = pl.program_id(0)

        def emit(src_ref):
            x = src_ref[...]
            x_ref[...] = x
            h_ref[...] = _rmsnorm_rows(x, g_ref[...]).astype(BF16)

        @pl.when(i < ta)
        def _():
            emit(xa_ref)

        @pl.when(i >= ta)
        def _():
            emit(xb_ref)

    row = pl.BlockSpec((tm, d), lambda i: (i, 0))
    return pl.pallas_call(
        body, grid=((na + nb) // tm,),
        in_specs=[pl.BlockSpec((tm, d), lambda i: (jnp.minimum(i, ta - 1), 0)),
                  pl.BlockSpec((tm, d), lambda i: (jnp.maximum(i - ta, 0), 0)),
                  pl.BlockSpec((1, d), lambda i: (0, 0))],
        out_specs=(row, row),
        out_shape=(jax.ShapeDtypeStruct((na + nb, d), BF16), jax.ShapeDtypeStruct((na + nb, d), F32)),
        compiler_params=_cparams(("arbitrary",)), name="rmsnorm_concat")(xa, xb, g)


def _inproj_call(h, w, tm, tn):
    n, k = h.shape
    nc = w.shape[1]

    def body(h_ref, w_ref, o_ref):
        o_ref[...] = jnp.dot(h_ref[...], w_ref[...], preferred_element_type=F32)

    return pl.pallas_call(
        body, grid=(nc // tn, n // tm),
        in_specs=[pl.BlockSpec((tm, k), lambda j, i: (i, 0)), pl.BlockSpec((k, tn), lambda j, i: (0, j))],
        out_specs=pl.BlockSpec((tm, tn), lambda j, i: (i, j)),
        out_shape=jax.ShapeDtypeStruct((n, nc), F32),
        compiler_params=_cparams(("parallel", "arbitrary")), name="inproj")(h, w)


def _outproj_call(x, merged, w_out, g, tm, last):
    n, d = x.shape

    def body(x_ref, m_ref, w_ref, g_ref, *o_refs):
        y = x_ref[...] + jnp.dot(m_ref[...], w_ref[...], preferred_element_type=F32)
        hn = _rmsnorm_rows(y, g_ref[...])
        if last:
            o_refs[0][...] = hn
        else:
            o_refs[0][...] = y
            o_refs[1][...] = hn.astype(BF16)

    row = pl.BlockSpec((tm, d), lambda i: (i, 0))
    if last:
        out_specs, out_shape = row, jax.ShapeDtypeStruct((n, d), F32)
    else:
        out_specs = (row, row)
        out_shape = (jax.ShapeDtypeStruct((n, d), F32), jax.ShapeDtypeStruct((n, d), BF16))
    return pl.pallas_call(
        body, grid=(n // tm,),
        in_specs=[row, row, pl.BlockSpec((d, d), lambda i: (0, 0)), pl.BlockSpec((1, d), lambda i: (0, 0))],
        out_specs=out_specs, out_shape=out_shape,
        compiler_params=_cparams(("parallel",)), name="outproj")(x, merged, w_out, g)


def _gelu_tanh(x):
    return 0.5 * x * (1.0 + jnp.tanh(math.sqrt(2.0 / math.pi) * (x + 0.044715 * (x * x * x))))


def _finalize_call(proj, lru_h, rw_y, rw_bonus, ret_o, s5_y, fprm, glu_w, ones_bd, tm):
    n = proj.shape[0]
    w = W_BR

    def body(lz_ref, rz_ref, cz_ref, su_ref, sz_ref, lh_ref, ry_ref, rb_ref, co_ref, sy_ref, p_ref, glu_ref,
             ones_ref, o_ref):
        def both(ref, sl=slice(None)):
            return ref[0, :, sl].astype(F32) + ref[1, :, sl].astype(F32)

        o_ref[0] = (both(lh_ref) * _silu(lz_ref[...])).astype(BF16)

        y = both(ry_ref)
        ones = ones_ref[...]
        yc = y - _split_dot(y, ones) * (1.0 / RWKV_HEAD)
        var = _split_dot(yc * yc, ones) * (1.0 / RWKV_HEAD)
        yb = yc * lax.rsqrt(var + RWKV_LN_EPS) * p_ref[0:1, :] + p_ref[1:2, :] + rb_ref[...].astype(F32)
        o_ref[1] = (yb * _silu(rz_ref[...])).astype(BF16)

        for h in range(RET_H):
            sl = slice(RET_DV * h, RET_DV * (h + 1))
            oh = both(co_ref, sl)
            oc = oh - jnp.mean(oh, axis=-1, keepdims=True)
            ov = jnp.mean(oc * oc, axis=-1, keepdims=True)
            yc_h = oc * lax.rsqrt(ov + RET_GN_EPS) * p_ref[2:3, sl]
            o_ref[2, :, sl] = (yc_h * _silu(cz_ref[:, sl])).astype(BF16)

        s = _gelu_tanh(both(sy_ref) + p_ref[3:4, :] * su_ref[...])
        s = s * _sigmoid(_bdot(s, glu_ref[...]) + p_ref[4:5, :])
        o_ref[3] = (s * _silu(sz_ref[...])).astype(BF16)

    def col(name):
        cb = COL[name] // w
        return pl.BlockSpec((tm, w), lambda i: (i, cb))

    both = pl.BlockSpec((2, tm, w), lambda i: (0, i, 0))
    first = pl.BlockSpec((None, tm, w), lambda i: (0, i, 0))

    def const(shape):
        return pl.BlockSpec(tuple(shape), lambda i: (0,) * len(shape))

    return pl.pallas_call(
        body, grid=(n // tm,),
        in_specs=[col("lru_z"), col("rw_z"), col("ret_z"), col("s5_u"), col("s5_z"), both, both, first, both, both,
                  const(fprm.shape), const(glu_w.shape), const(ones_bd.shape)],
        out_specs=pl.BlockSpec((N_BRANCH, tm, w), lambda i: (0, i, 0)),
        out_shape=jax.ShapeDtypeStruct((N_BRANCH, n, w), BF16),
        compiler_params=_cparams(("parallel",)), name="finalize")(
            proj, proj, proj, proj, proj, lru_h, rw_y, rw_bonus, ret_o, s5_y, fprm, glu_w, ones_bd)


def _merge_call(ycat, proj, w_branch, tm, tn):
    n = proj.shape[0]
    gate0 = COL["gates"] // tn
    per_branch = D_MODEL // tn

    def body(y_ref, g0, g1, g2, g3, w_ref, o_ref):
        acc = None
        for b, g_ref in enumerate((g0, g1, g2, g3)):
            term = _sigmoid(g_ref[...]) * jnp.dot(y_ref[b], w_ref[b], preferred_element_type=F32)
            acc = term if acc is None else acc + term
        o_ref[...] = acc.astype(BF16)

    gates = [pl.BlockSpec((tm, tn), lambda j, i, b=b: (i, gate0 + b * per_branch + j)) for b in range(N_BRANCH)]
    return pl.pallas_call(
        body, grid=(D_MODEL // tn, n // tm),
        in_specs=[pl.BlockSpec((N_BRANCH, tm, W_BR), lambda j, i: (0, i, 0)), *gates,
                  pl.BlockSpec((N_BRANCH, W_BR, tn), lambda j, i: (0, 0, j))],
        out_specs=pl.BlockSpec((tm, tn), lambda j, i: (i, j)),
        out_shape=jax.ShapeDtypeStruct((n, D_MODEL), BF16),
        compiler_params=_cparams(("parallel", "arbitrary")), name="merge")(
            ycat, proj, proj, proj, proj, w_branch)


def _seq_flags(seq_lens, t):
    sf, ef = [], []
    for ln in seq_lens:
        assert ln % t == 0
        k = ln // t
        sf += [1] + [0] * (k - 1)
        ef += [0] * (k - 1) + [1]
    return jnp.asarray(sf, jnp.int32), jnp.asarray(ef, jnp.int32)


def _tile_of(d, i, n_t):
    return i + d * (n_t - 1 - 2 * i)


def _tile_spec(t, w, col, n_t):
    cb = col // w
    return pl.BlockSpec((t, w), lambda d, i, sf, ef: (_tile_of(d, i, n_t), cb))


def _halo_specs(t, w, col, n_t):
    cb = col // w
    hb = t // SUBLANES
    last = n_t * hb - 1
    prev = pl.BlockSpec((SUBLANES, w), lambda d, i, sf, ef: (jnp.maximum(_tile_of(d, i, n_t) * hb - 1, 0), cb))
    nxt = pl.BlockSpec((SUBLANES, w), lambda d, i, sf, ef: (jnp.minimum((_tile_of(d, i, n_t) + 1) * hb, last), cb))
    return prev, nxt


def _dir_spec(shape):
    nd = len(shape)
    return pl.BlockSpec((None,) + tuple(shape[1:]), lambda d, i, sf, ef: (d,) + (0,) * (nd - 1))


def _const_spec(shape):
    nd = len(shape)
    return pl.BlockSpec(tuple(shape), lambda d, i, sf, ef: (0,) * nd)


def _fill_halo(buf, x_ref, xp_ref, xn_ref, keep_p, keep_n, t):
    buf[0:SUBLANES, :] = xp_ref[...] * keep_p
    buf[SUBLANES:t + SUBLANES, :] = x_ref[...]
    buf[t + SUBLANES:t + 2 * SUBLANES, :] = xn_ref[...] * keep_n


def _seg_pitch(t):
    return t // SUBLANES + 4


def _seg_rows(t):
    return SUBLANES * _seg_pitch(t)


def _to_segments(ref, blk, val, t):
    s_len, pitch = t // SUBLANES, _seg_pitch(t)
    for k in range(SUBLANES):
        ref[blk, pitch * k:pitch * k + s_len, :] = val[s_len * k:s_len * (k + 1), :]


def _from_segments(ref, blk, t):
    s_len, pitch = t // SUBLANES, _seg_pitch(t)
    return jnp.concatenate([ref[blk, pitch * k:pitch * k + s_len, :] for k in range(SUBLANES)], axis=0)


def _seg_scan_real(a_ref, b_ref, o_ref, carry_ref, t, rev):
    s_len = t // SUBLANES
    pitch = _seg_pitch(t)
    nb = a_ref.shape[0]
    blocks = range(nb)

    def ld(ref, cb, s):
        return ref[cb, pl.ds(s, SUBLANES, stride=pitch), :]

    def step_of(j):
        return s_len - 1 - j if rev else j

    def pass1(j, hp):
        hs, ps = hp
        s = step_of(j)
        a = [ld(a_ref, cb, s) for cb in blocks]
        return (tuple(a[cb] * hs[cb] + ld(b_ref, cb, s) for cb in blocks),
                tuple(a[cb] * ps[cb] for cb in blocks))

    zero = jnp.zeros((SUBLANES, LANES), F32)
    es, ps = lax.fori_loop(0, s_len, pass1, ((zero,) * nb, (zero + 1.0,) * nb), unroll=2)
    starts = []
    for cb in blocks:
        c = carry_ref[cb, 0:1, :]
        rows = [None] * SUBLANES
        for k in (range(SUBLANES - 1, -1, -1) if rev else range(SUBLANES)):
            rows[k] = c
            c = ps[cb][k:k + 1, :] * c + es[cb][k:k + 1, :]
        carry_ref[cb, 0:1, :] = c
        starts.append(jnp.concatenate(rows, axis=0))

    def pass2(j, hs):
        s = step_of(j)
        out = []
        for cb in blocks:
            h = ld(a_ref, cb, s) * hs[cb] + ld(b_ref, cb, s)
            o_ref[cb, pl.ds(s, SUBLANES, stride=pitch), :] = h
            out.append(h)
        return tuple(out)

    lax.fori_loop(0, s_len, pass2, tuple(starts), unroll=2)


def _lru_call(proj, conv_w, conv_b, wcat, bcat, sp, flags, t):
    n = proj.shape[0]
    n_t = n // t
    w = W_BR

    def body(sf_ref, ef_ref, x_ref, xp_ref, xn_ref, cw_ref, cb_ref, w_ref, b_ref, sp_ref, o_ref,
             xbuf, a_scr, b_scr, h_scr, carry):
        d = pl.program_id(0)
        ti = _tile_of(d, pl.program_id(1), n_t)
        sf = sf_ref[ti]
        ef = ef_ref[ti]
        _fill_halo(xbuf, x_ref, xp_ref, xn_ref, (1 - sf).astype(F32), (1 - ef).astype(F32), t)
        o = SUBLANES
        xc = (cw_ref[0:1, :] * xbuf[o - 2:o - 2 + t, :] + cw_ref[1:2, :] * xbuf[o - 1:o - 1 + t, :]
              + cw_ref[2:3, :] * xbuf[o:o + t, :] + cw_ref[3:4, :] * xbuf[o + 1:o + 1 + t, :] + cb_ref[...])
        row = lax.broadcasted_iota(jnp.int32, (t, LRU_BS), 0)
        first_row = jnp.where(d == 0, 0, t - 1)
        at_seq_edge = jnp.where(d == 0, sf, ef) == 1
        first = jnp.logical_and(row == first_row, at_seq_edge)
        for hb in range(LRU_BLOCKS):
            sl = slice(LRU_BS * hb, LRU_BS * (hb + 1))
            xcb = xc[:, sl]
            g = _bdot(xcb, w_ref[hb])
            r = _sigmoid(g[:, :LRU_BS] + b_ref[0:1, sl])
            ig = _sigmoid(g[:, LRU_BS:] + b_ref[1:2, sl])
            log_a = (-LRU_C) * r * _softplus(-sp_ref[0:1, sl])
            mult = jnp.sqrt(_one_minus_exp(2.0 * log_a))
            mult = jnp.where(first, 1.0, mult)
            _to_segments(a_scr, hb, jnp.exp(log_a), t)
            _to_segments(b_scr, hb, mult * ig * xcb, t)

        @pl.when(at_seq_edge)
        def _():
            carry[...] = jnp.zeros_like(carry)

        @pl.when(d == 0)
        def _():
            _seg_scan_real(a_scr, b_scr, h_scr, carry, t, False)

        @pl.when(d == 1)
        def _():
            _seg_scan_real(a_scr, b_scr, h_scr, carry, t, True)

        for hb in range(LRU_BLOCKS):
            o_ref[:, LRU_BS * hb:LRU_BS * (hb + 1)] = _from_segments(h_scr, hb, t).astype(SWEEP_OUT)

    prev, nxt = _halo_specs(t, w, COL["lru_x"], n_t)
    grid_spec = pltpu.PrefetchScalarGridSpec(
        num_scalar_prefetch=2, grid=(2, n_t),
        in_specs=[_tile_spec(t, w, COL["lru_x"], n_t), prev, nxt,
                  _const_spec(conv_w.shape), _const_spec(conv_b.shape),
                  _dir_spec(wcat.shape), _dir_spec(bcat.shape), _dir_spec(sp.shape)],
        out_specs=pl.BlockSpec((None, t, w), lambda d, i, sf, ef: (d, _tile_of(d, i, n_t), 0)),
        scratch_shapes=[pltpu.VMEM((t + 2 * SUBLANES, w), F32)]
        + [pltpu.VMEM((LRU_BLOCKS, _seg_rows(t), LRU_BS), F32)] * 3
        + [pltpu.VMEM((LRU_BLOCKS, SUBLANES, LRU_BS), F32)])
    return pl.pallas_call(
        body, grid_spec=grid_spec, out_shape=jax.ShapeDtypeStruct((2, n, w), SWEEP_OUT),
        compiler_params=_cparams(("arbitrary", "arbitrary")), name="lru_sweep")(
            *flags, proj, proj, proj, conv_w, conv_b, wcat, bcat, sp)


def _prep_lru(conv_w, conv_b, w_r, b_r, w_i, b_i, lam):
    wcat = jnp.concatenate([w_r, w_i], axis=-1).astype(BF16)
    bcat = jnp.stack([b_r, b_i], axis=1)
    return conv_w, conv_b.reshape(1, W_BR), wcat, bcat, lam.reshape(2, 1, W_BR)


_NT3 = (((2,), (2,)), ((0,), (0,)))
_NN3 = (((2,), (1,)), ((0,), (0,)))


def _nt(a, b):
    return lax.dot_general(a.astype(BF16), b.astype(BF16), _NT3, preferred_element_type=F32)


def _nn(a, b):
    return lax.dot_general(a.astype(BF16), b.astype(BF16), _NN3, preferred_element_type=F32)


def _rwkv_call(proj, prm, w0, w2, a2, ones_bd, tri, flags, t):
    n = proj.shape[0]
    n_t = n // t
    w = W_BR
    c = RWKV_CHUNK
    n_c = t // c
    nh, hd = RWKV_H, RWKV_HEAD

    def body(sf_ref, ef_ref, r_ref, rp_ref, rn_ref, k_ref, kp_ref, kn_ref, v_ref, vp_ref, vn_ref, lora_ref,
             prm_ref, w0_ref, w2_ref, a2_ref, ones_ref, tri_ref, y_ref, bonus_ref,
             xbuf, art_s, bk_s, vt_s, top_s, bot_s, tinv_s, yt_s, gam_s, state):
        d = pl.program_id(0)
        ti = _tile_of(d, pl.program_id(1), n_t)
        sf = sf_ref[ti]
        ef = ef_ref[ti]
        keep_p, keep_n = (1 - sf).astype(F32), (1 - ef).astype(F32)
        o = SUBLANES

        def mixed(x_ref, xp_ref, xn_ref, mu):
            _fill_halo(xbuf, x_ref, xp_ref, xn_ref, keep_p, keep_n, t)
            x = xbuf[o:o + t, :]
            return x + mu * (0.5 * (xbuf[o - 1:o - 1 + t, :] + xbuf[o + 1:o + 1 + t, :]) - x)

        r = mixed(r_ref, rp_ref, rn_ref, prm_ref[0:1, :])
        k = mixed(k_ref, kp_ref, kn_ref, prm_ref[1:2, :])
        v = mixed(v_ref, vp_ref, vn_ref, prm_ref[2:3, :])
        lora = lora_ref[...]
        wd = jnp.where(d == 0, lora[:, 0:RWKV_LORA], lora[:, RWKV_LORA:2 * RWKV_LORA])
        ad = lora[:, 2 * RWKV_LORA:3 * RWKV_LORA]
        a_icl = _sigmoid(prm_ref[3:4, :] + _bdot(ad, a2_ref[...]))
        w_log = -_softplus(-(w0_ref[...] + _bdot(jnp.tanh(wd), w2_ref[...]))) - 0.5
        logw = -jnp.exp(w_log)
        kkr = k * prm_ref[4:5, :]
        kk = kkr / jnp.maximum(jnp.sqrt(_bdot(kkr * kkr, ones_ref[...])), 1e-12)
        k_mod = k * (1.0 + (a_icl - 1.0) * prm_ref[5:6, :])
        bonus_ref[...] = (_split_dot(r * k_mod * prm_ref[6:7, :], ones_ref[...]) * v).astype(SWEEP_OUT)
        b_vec = kk * a_icl

        ii = lax.broadcasted_iota(jnp.int32, (c, 2 * c), 0)
        jj = lax.broadcasted_iota(jnp.int32, (c, 2 * c), 1)
        jj = jnp.where(jj >= c, jj - c, jj)
        order = (1 - 2 * d) * (ii - jj)
        strict = (order > 0)[None]
        incl = (order >= 0)[None]
        eye = (lax.broadcasted_iota(jnp.int32, (c, c), 0)
               == lax.broadcasted_iota(jnp.int32, (c, c), 1)).astype(F32)[None]

        vt3 = v.T.reshape(nh, hd, t)
        tri_m = tri_ref[...]
        for cc in range(n_c):
            rows = slice(c * cc, c * (cc + 1))
            lw = logw[rows, :]
            hi = lw.astype(BF16)
            lo = (lw - hi.astype(F32)).astype(BF16)
            g = jnp.dot(tri_m, hi, preferred_element_type=F32) + jnp.dot(tri_m, lo, preferred_element_type=F32)
            e_pos = jnp.exp(g)
            e_neg = jnp.exp(-g)
            at_c = -kk[rows, :] * jnp.exp(g - lw)
            rt_c = r[rows, :] * e_pos
            bt_c = b_vec[rows, :] * e_neg
            kt_c = k_mod[rows, :] * e_neg
            g_end = jnp.exp(jnp.where(d == 0, g[c - 1:c, :], g[0:1, :]))
            for h in range(nh):
                ls = slice(hd * h, hd * (h + 1))
                art_s[cc, h, 0:c, :] = at_c[:, ls]
                art_s[cc, h, c:2 * c, :] = rt_c[:, ls]
                bk_s[cc, h, 0:c, :] = bt_c[:, ls]
                bk_s[cc, h, c:2 * c, :] = kt_c[:, ls]
                gam_s[h, SUBLANES * cc:SUBLANES * (cc + 1), :] = jnp.broadcast_to(g_end[:, ls], (SUBLANES, hd))
            vt_c = vt3[:, :, rows]
            vt_s[cc] = jnp.concatenate([jnp.zeros_like(vt_c), vt_c], axis=2)

        @pl.when(jnp.where(d == 0, sf, ef) == 1)
        def _():
            state[...] = jnp.zeros_like(state)

        def chunk_of(j):
            return jnp.where(d == 0, j, n_c - 1 - j)

        def prepare(p, carry):
            pair = pl.ds(pl.multiple_of(2 * p, 2), 2)
            a_all = _nt(art_s[pair].reshape(2 * nh, 2 * c, hd), bk_s[pair].reshape(2 * nh, 2 * c, hd))
            top = jnp.where(strict, a_all[:, :c], 0.0)
            a_ab = top[:, :, :c]
            tinv = eye + a_ab
            pw = a_ab
            for _ in range(int(math.log2(c)) - 1):
                pw = _nn(pw, pw)
                tinv = tinv + _nn(pw, tinv)
            top_s[pair] = top.reshape(2, nh, c, 2 * c)
            bot_s[pair] = jnp.where(incl, a_all[:, c:], 0.0).reshape(2, nh, c, 2 * c)
            tinv_s[pair] = tinv.reshape(2, nh, c, c)
            return carry

        def advance(j, carry):
            cc = chunk_of(j)
            art = art_s[cc]
            bk = bk_s[cc]
            at, rt = art[:, :c], art[:, c:]
            vt_hi = vt_s[cc]
            gam = gam_s[:, pl.ds(pl.multiple_of(cc * SUBLANES, SUBLANES), 1), :]
            s0 = state[...]
            zt = _nt(s0, at) + _nt(vt_hi, top_s[cc])
            ut = _nt(zt, tinv_s[cc])
            ut_vt = jnp.concatenate([ut, jnp.zeros_like(ut)], axis=2) + vt_hi
            yt_s[cc] = _nt(s0, rt) + _nt(ut_vt, bot_s[cc])
            state[...] = (s0 + _nn(ut_vt, bk)) * gam
            return carry

        lax.fori_loop(0, n_c // 2, prepare, 0)
        lax.fori_loop(0, n_c, advance, 0)
        yt = jnp.concatenate([yt_s[cc] for cc in range(n_c)], axis=2)
        y_ref[...] = yt.reshape(w, t).T.astype(SWEEP_OUT)

    lora_col = COL["rw_wdf"]
    r_halo = _halo_specs(t, w, COL["rw_r"], n_t)
    k_halo = _halo_specs(t, w, COL["rw_k"], n_t)
    v_halo = _halo_specs(t, w, COL["rw_v"], n_t)
    stacked = pltpu.VMEM((n_c, nh, 2 * c, hd), F32)
    paired = pltpu.VMEM((n_c, nh, hd, 2 * c), F32)
    grid_spec = pltpu.PrefetchScalarGridSpec(
        num_scalar_prefetch=2, grid=(2, n_t),
        in_specs=[_tile_spec(t, w, COL["rw_r"], n_t), *r_halo, _tile_spec(t, w, COL["rw_k"], n_t), *k_halo,
                  _tile_spec(t, w, COL["rw_v"], n_t), *v_halo, _tile_spec(t, 4 * RWKV_LORA, lora_col, n_t),
                  _const_spec(prm.shape), _dir_spec(w0.shape), _dir_spec(w2.shape), _const_spec(a2.shape),
                  _const_spec(ones_bd.shape), _dir_spec(tri.shape)],
        out_specs=(pl.BlockSpec((None, t, w), lambda d, i, sf, ef: (d, _tile_of(d, i, n_t), 0)),
                   pl.BlockSpec((None, t, w), lambda d, i, sf, ef: (d, _tile_of(d, i, n_t), 0))),
        scratch_shapes=[pltpu.VMEM((t + 2 * SUBLANES, w), F32), stacked, stacked,
                        paired, paired, paired, pltpu.VMEM((n_c, nh, c, c), F32),
                        pltpu.VMEM((n_c, nh, hd, c), F32),
                        pltpu.VMEM((nh, n_c * SUBLANES, hd), F32),
                        pltpu.VMEM((nh, hd, hd), F32)])
    return pl.pallas_call(
        body, grid_spec=grid_spec,
        out_shape=(jax.ShapeDtypeStruct((2, n, w), SWEEP_OUT), jax.ShapeDtypeStruct((2, n, w), SWEEP_OUT)),
        compiler_params=_cparams(("arbitrary", "arbitrary")), name="rwkv_sweep")(
            *flags, proj, proj, proj, proj, proj, proj, proj, proj, proj, proj, prm, w0, w2, a2, ones_bd, tri)


def _prep_rwkv(mu, w0, w2, a0, a2, k_k, k_a, r_k):
    prm = jnp.concatenate([mu, a0[None], k_k[None], k_a[None], r_k.reshape(1, W_BR),
                           jnp.zeros((1, W_BR), F32)], axis=0)
    head = np.arange(W_BR) // RWKV_HEAD
    ones_bd = jnp.asarray(head[:, None] == head[None, :], BF16)
    i = np.arange(RWKV_CHUNK)
    tri = jnp.asarray(np.stack([i[:, None] >= i[None, :], i[:, None] <= i[None, :]]), BF16)
    return prm, w0.reshape(2, 1, W_BR), w2.astype(BF16), a2.astype(BF16), ones_bd, tri


def _head_norm_ref(y, eps, hd):
    yh = y.reshape(y.shape[0], -1, hd)
    mean = jnp.mean(yh, axis=-1, keepdims=True)
    yc = yh - mean
    var = jnp.mean(yc * yc, axis=-1, keepdims=True)
    return (yc * lax.rsqrt(var + eps)).reshape(y.shape)


def _ret_tables(seq_lens):
    half = RET_DK // 2
    pos = jnp.concatenate([jnp.arange(ln, dtype=F32) for ln in seq_lens])
    inv = ROPE_BASE ** (-jnp.arange(half, dtype=F32) / half)
    ang = pos[:, None] * inv[None, :]
    cos, sin = jnp.cos(ang), jnp.sin(ang)
    cos_t = jnp.concatenate([cos, cos], axis=1)
    sin_t = jnp.concatenate([-sin, sin], axis=1)
    c = RET_CHUNK
    log_g = np.log(1.0 - 2.0 ** (-5.0 - np.arange(RET_H, dtype=np.float64)))
    i = np.arange(c, dtype=np.float64)
    dmat = np.exp(log_g[:, None, None] * np.abs(i[:, None] - i[None, :]))
    rows = np.stack([np.exp(log_g[:, None] * (i + 1.0)), np.exp(log_g[:, None] * (c - i)),
                     np.exp(log_g[:, None] * (c - 1.0 - i)), np.exp(log_g[:, None] * i)], axis=1)
    rows = np.broadcast_to(rows[..., None], (RET_H, 4, c, RET_DK))
    return cos_t, sin_t, jnp.asarray(dmat, F32), jnp.asarray(rows, F32)


def _ret_call(proj, cos_t, sin_t, dmat, rowsc, flags, t):
    n = proj.shape[0]
    n_t = n // t
    c = RET_CHUNK
    n_c = t // c
    g_chunk = [float((1.0 - 2.0 ** (-5.0 - h)) ** c) for h in range(RET_H)]
    tn_dims = (((0,), (0,)), ((), ()))
    nt_dims = (((1,), (1,)), ((), ()))

    def body(sf_ref, ef_ref, q_ref, k_ref, v_ref, cos_ref, sin_ref, dm_ref, rs_ref, o_ref, state):
        d = pl.program_id(0)
        ti = _tile_of(d, pl.program_id(1), n_t)
        at_seq_edge = jnp.where(d == 0, sf_ref[ti], ef_ref[ti]) == 1

        @pl.when(at_seq_edge)
        def _():
            state[...] = jnp.zeros_like(state)

        def rot(x_ref, rows, h):
            x = x_ref[rows, RET_DK * h:RET_DK * (h + 1)]
            return x * cos_ref[rows, :] + pltpu.roll(x, RET_DK // 2, 1) * sin_ref[rows, :]

        def chunk(cc, fwd):
            rows = slice(c * cc, c * (cc + 1))
            for h in range(RET_H):
                qh = rot(q_ref, rows, h)
                kh = rot(k_ref, rows, h) * (RET_DK ** -0.5)
                vh = v_ref[rows, RET_DV * h:RET_DV * (h + 1)].astype(BF16)
                s_old = state[h]
                if fwd:
                    sc = lax.dot_general(qh.astype(BF16), kh.astype(BF16), nt_dims,
                                         preferred_element_type=F32) * dm_ref[h]
                    o = (jnp.dot(sc.astype(BF16), vh, preferred_element_type=F32)
                         + _bdot(qh * rs_ref[h, 0], s_old))
                    kd = kh * rs_ref[h, 2]
                else:
                    o = _bdot(qh * rs_ref[h, 1], s_old)
                    kd = kh * rs_ref[h, 3]
                o_ref[rows, RET_DV * h:RET_DV * (h + 1)] = o.astype(SWEEP_OUT)
                state[h] = g_chunk[h] * s_old + lax.dot_general(kd.astype(BF16), vh, tn_dims,
                                                                 preferred_element_type=F32)

        @pl.when(d == 0)
        def _():
            for cc in range(n_c):
                chunk(cc, True)

        @pl.when(d == 1)
        def _():
            for cc in range(n_c - 1, -1, -1):
                chunk(cc, False)

    hk = RET_H * RET_DK
    tab = pl.BlockSpec((t, RET_DK), lambda d, i, sf, ef: (_tile_of(d, i, n_t), 0))
    grid_spec = pltpu.PrefetchScalarGridSpec(
        num_scalar_prefetch=2, grid=(2, n_t),
        in_specs=[_tile_spec(t, hk, COL["ret_q"], n_t), _tile_spec(t, hk, COL["ret_k"], n_t),
                  _tile_spec(t, W_BR, COL["ret_v"], n_t), tab, tab,
                  _const_spec(dmat.shape), _const_spec(rowsc.shape)],
        out_specs=pl.BlockSpec((None, t, W_BR), lambda d, i, sf, ef: (d, _tile_of(d, i, n_t), 0)),
        scratch_shapes=[pltpu.VMEM((RET_H, RET_DK, RET_DV), F32)])
    return pl.pallas_call(
        body, grid_spec=grid_spec, out_shape=jax.ShapeDtypeStruct((2, n, W_BR), SWEEP_OUT),
        compiler_params=_cparams(("arbitrary", "arbitrary")), name="ret_sweep")(
            *flags, proj, proj, proj, cos_t, sin_t, dmat, rowsc)


def _seg_scan_complex(br_ref, bi_ref, sr_ref, si_ref, carry_r, carry_i, base, lam_ref, t, rev):
    s_len = t // SUBLANES
    pitch = _seg_pitch(t)
    nb = br_ref.shape[0]
    blocks = range(nb)
    lr = [jnp.broadcast_to(lam_ref[0:1, LANES * cb:LANES * (cb + 1)], (SUBLANES, LANES)) for cb in blocks]
    li = [jnp.broadcast_to(lam_ref[1:2, LANES * cb:LANES * (cb + 1)], (SUBLANES, LANES)) for cb in blocks]

    def ld(ref, cb, s):
        return ref[cb, pl.ds(s, SUBLANES, stride=pitch), :]

    def step_of(j):
        return s_len - 1 - j if rev else j

    def advance(cb, s, hr, hi):
        nr = lr[cb] * hr - li[cb] * hi + ld(br_ref, cb, s)
        ni = lr[cb] * hi + li[cb] * hr + ld(bi_ref, cb, s)
        return nr, ni

    def pass1(j, h):
        s = step_of(j)
        return tuple(advance(cb, s, *h[cb]) for cb in blocks)

    zero = jnp.zeros((SUBLANES, LANES), F32)
    ends = lax.fori_loop(0, s_len, pass1, ((zero, zero),) * nb, unroll=2)
    starts = []
    for cb in blocks:
        sl = slice(LANES * cb, LANES * (cb + 1))
        pr, pi = lam_ref[2:3, sl], lam_ref[3:4, sl]
        cr, ci = carry_r[base + cb, 0:1, :], carry_i[base + cb, 0:1, :]
        rows_r, rows_i = [None] * SUBLANES, [None] * SUBLANES
        for k in (range(SUBLANES - 1, -1, -1) if rev else range(SUBLANES)):
            rows_r[k], rows_i[k] = cr, ci
            er, ei = ends[cb][0][k:k + 1, :], ends[cb][1][k:k + 1, :]
            cr, ci = pr * cr - pi * ci + er, pr * ci + pi * cr + ei
        carry_r[base + cb, 0:1, :] = cr
        carry_i[base + cb, 0:1, :] = ci
        starts.append((jnp.concatenate(rows_r, axis=0), jnp.concatenate(rows_i, axis=0)))

    def pass2(j, h):
        s = step_of(j)
        out = []
        for cb in blocks:
            nr, ni = advance(cb, s, *h[cb])
            sr_ref[cb, pl.ds(s, SUBLANES, stride=pitch), :] = nr
            si_ref[cb, pl.ds(s, SUBLANES, stride=pitch), :] = ni
            out.append((nr, ni))
        return tuple(out)

    lax.fori_loop(0, s_len, pass2, tuple(starts), unroll=2)


def _s5_call(proj, bblk, cblk, lam4, flags, t):
    n = proj.shape[0]
    n_t = n // t
    w = W_BR
    lb = S5_SW // LANES

    def body(sf_ref, ef_ref, u_ref, b_ref, c_ref, lam_ref, o_ref, br, bi, sr, si, carry_r, carry_i):
        d = pl.program_id(0)
        ti = _tile_of(d, pl.program_id(1), n_t)
        at_seq_edge = jnp.where(d == 0, sf_ref[ti], ef_ref[ti]) == 1

        @pl.when(at_seq_edge)
        def _():
            carry_r[...] = jnp.zeros_like(carry_r)
            carry_i[...] = jnp.zeros_like(carry_i)

        for ob in range(S5_NB):
            bu = _bdot(u_ref[:, LANES * ob:LANES * (ob + 1)], b_ref[ob])
            for cb in range(lb):
                _to_segments(br, cb, bu[:, LANES * cb:LANES * (cb + 1)], t)
                _to_segments(bi, cb, bu[:, S5_SW + LANES * cb:S5_SW + LANES * (cb + 1)], t)

            @pl.when(d == 0)
            def _():
                _seg_scan_complex(br, bi, sr, si, carry_r, carry_i, ob * lb, lam_ref.at[ob], t, False)

            @pl.when(d == 1)
            def _():
                _seg_scan_complex(br, bi, sr, si, carry_r, carry_i, ob * lb, lam_ref.at[ob], t, True)

            st = jnp.concatenate([_from_segments(sr, cb, t) for cb in range(lb)]
                                 + [_from_segments(si, cb, t) for cb in range(lb)], axis=1)
            o_ref[:, LANES * ob:LANES * (ob + 1)] = _bdot(st, c_ref[ob]).astype(SWEEP_OUT)

    grid_spec = pltpu.PrefetchScalarGridSpec(
        num_scalar_prefetch=2, grid=(2, n_t),
        in_specs=[_tile_spec(t, w, COL["s5_u"], n_t), _const_spec(bblk.shape), _dir_spec(cblk.shape),
                  _dir_spec(lam4.shape)],
        out_specs=pl.BlockSpec((None, t, w), lambda d, i, sf, ef: (d, _tile_of(d, i, n_t), 0)),
        scratch_shapes=[pltpu.VMEM((lb, _seg_rows(t), LANES), F32)] * 4
        + [pltpu.VMEM((S5_NB * lb, SUBLANES, LANES), F32)] * 2)
    return pl.pallas_call(
        body, grid_spec=grid_spec, out_shape=jax.ShapeDtypeStruct((2, n, w), SWEEP_OUT),
        compiler_params=_cparams(("arbitrary", "arbitrary")), name="s5_sweep")(
            *flags, proj, bblk, cblk, lam4)


def _prep_s5(lam_re, lam_im, log_step, b_re, b_im, c_re, c_im, t):
    s_len = t // SUBLANES
    assert s_len & (s_len - 1) == 0
    lre = jnp.minimum(lam_re, LAM_RE_MAX)
    step = jnp.exp(log_step)[..., None]
    mag = jnp.exp(lre * step)
    lbr, lbi = mag * jnp.cos(lam_im * step), mag * jnp.sin(lam_im * step)
    den = lre * lre + lam_im * lam_im
    gr = ((lbr - 1.0) * lre + lbi * lam_im) / den
    gi = (lbi * lre - (lbr - 1.0) * lam_im) / den
    pr, pi = lbr, lbi
    for _ in range(int(math.log2(s_len))):
        pr, pi = pr * pr - pi * pi, 2.0 * pr * pi
    lam4 = jnp.stack([x.reshape(2, S5_NB, S5_SW) for x in (lbr, lbi, pr, pi)], axis=2)
    cpr = c_re[None] * gr[:, :, None, :] - c_im[None] * gi[:, :, None, :]
    cpi = c_re[None] * gi[:, :, None, :] + c_im[None] * gr[:, :, None, :]
    eye = jnp.eye(S5_GB, dtype=F32)

    def c_rows(x):
        x = x.reshape(2, S5_NB, S5_GB, S5_P, S5_N)
        return jnp.einsum('dogpn,gh->dognhp', x, eye).reshape(2, S5_NB, S5_SW, S5_GB * S5_P)

    cblk = jnp.concatenate([c_rows(cpr), c_rows(-cpi)], axis=2).astype(BF16)

    def b_cols(x):
        x = x.reshape(S5_NB, S5_GB, S5_N, S5_P)
        return jnp.einsum('ognp,gh->ogphn', x, eye).reshape(S5_NB, S5_GB * S5_P, S5_SW)

    bblk = jnp.concatenate([b_cols(b_re), b_cols(b_im)], axis=2).astype(BF16)
    return bblk, cblk, lam4


def _tile_plan(n, seq_lens):
    shortest = min(seq_lens)
    return dict(
        sweep=min(1024, shortest),
        rwkv=min(256, shortest),
        in_m=min(1024, n), in_n=1792,
        fin_m=min(256, n),
        merge_m=min(512, n), merge_n=1024,
        out_m=min(256, n),
        norm_m=min(512, n),
    )


def _permute_w_in(w_in):
    orig, start = {}, 0
    for name, nn in _ORIG_SPLITS:
        orig[name] = (start, nn)
        start += nn
    parts = [w_in[:, orig[name][0]:orig[name][0] + orig[name][1]] for name in _NEW_ORDER]
    parts.append(jnp.zeros((w_in.shape[0], _N_PAD), w_in.dtype))
    return jnp.concatenate(parts, axis=1).astype(BF16)


def _trunk(x, seq_lens, norm_g, w_in, lru_conv_w, lru_conv_b, lru_w_r, lru_b_r, lru_w_i, lru_b_i, lru_lambda,
           rwkv_mu, rwkv_w0, rwkv_w2, rwkv_a0, rwkv_a2, rwkv_k_k, rwkv_k_a, rwkv_r_k, rwkv_lnx_g, rwkv_lnx_b,
           ret_gn_g, s5_lam_re, s5_lam_im, s5_log_step, s5_b_re, s5_b_im, s5_c_re, s5_c_im, s5_d, s5_glu_w,
           s5_glu_b, w_branch, w_out, final_g, h0=None):
    n = x.shape[0]
    tp = _tile_plan(n, seq_lens)
    flags = _seq_flags(seq_lens, tp["sweep"])
    flags_rw = _seq_flags(seq_lens, tp["rwkv"])
    ret_tabs = _ret_tables(seq_lens)
    depth = w_in.shape[0]
    h = _norm_call(x, norm_g[0].reshape(1, D_MODEL), tp["norm_m"]) if h0 is None else h0
    for l in range(depth):
        proj = _inproj_call(h, _permute_w_in(w_in[l]), tp["in_m"], tp["in_n"])
        lru_h = _lru_call(proj, *_prep_lru(lru_conv_w[l], lru_conv_b[l], lru_w_r[l], lru_b_r[l], lru_w_i[l],
                                           lru_b_i[l], lru_lambda[l]), flags, tp["sweep"])
        rw_prm = _prep_rwkv(rwkv_mu[l], rwkv_w0[l], rwkv_w2[l], rwkv_a0[l], rwkv_a2[l], rwkv_k_k[l], rwkv_k_a[l],
                            rwkv_r_k[l])
        rw_y, rw_bonus = _rwkv_call(proj, *rw_prm, flags_rw, tp["rwkv"])
        ret_o = _ret_call(proj, *ret_tabs, flags, tp["sweep"])
        s5_y = _s5_call(proj, *_prep_s5(s5_lam_re[l], s5_lam_im[l], s5_log_step[l], s5_b_re[l], s5_b_im[l],
                                        s5_c_re[l], s5_c_im[l], tp["sweep"]), flags, tp["sweep"])
        fprm = jnp.stack([rwkv_lnx_g[l], rwkv_lnx_b[l], ret_gn_g[l], s5_d[l], s5_glu_b[l],
                          jnp.zeros_like(s5_d[l]), jnp.zeros_like(s5_d[l]), jnp.zeros_like(s5_d[l])])
        ycat = _finalize_call(proj, lru_h, rw_y, rw_bonus, ret_o, s5_y, fprm, s5_glu_w[l].astype(BF16),
                              rw_prm[4], tp["fin_m"])
        merged = _merge_call(ycat, proj, w_branch[l].astype(BF16), tp["merge_m"], tp["merge_n"])
        last = l == depth - 1
        g_next = (final_g if last else norm_g[l + 1]).reshape(1, D_MODEL)
        res = _outproj_call(x, merged, w_out[l].astype(BF16), g_next, tp["out_m"], last)
        if last:
            return res
        x, h = res


def kernel(x_prompt, x_sample, norm_g, w_in, lru_conv_w, lru_conv_b, lru_w_r, lru_b_r, lru_w_i, lru_b_i, lru_lambda, rwkv_mu, rwkv_w0, rwkv_w2, rwkv_a0, rwkv_a2, rwkv_k_k, rwkv_k_a, rwkv_r_k, rwkv_lnx_g, rwkv_lnx_b, ret_gn_g, s5_lam_re, s5_lam_im, s5_log_step, s5_b_re, s5_b_im, s5_c_re, s5_c_im, s5_d, s5_glu_w, s5_glu_b, w_branch, w_out, final_g):
    bp, lp, d = x_prompt.shape
    bs, ls, _ = x_sample.shape
    seq_lens = (lp,) * bp + (ls,) * bs
    tm = _tile_plan(bp * lp + bs * ls, seq_lens)["norm_m"]
    h0, x = _norm_concat_call(x_prompt.reshape(bp * lp, d), x_sample.reshape(bs * ls, d),
                              norm_g[0].reshape(1, d), tm)
    y = _trunk(x, seq_lens, norm_g, w_in, lru_conv_w, lru_conv_b, lru_w_r, lru_b_r, lru_w_i, lru_b_i, lru_lambda,
               rwkv_mu, rwkv_w0, rwkv_w2, rwkv_a0, rwkv_a2, rwkv_k_k, rwkv_k_a, rwkv_r_k, rwkv_lnx_g, rwkv_lnx_b,
               ret_gn_g, s5_lam_re, s5_lam_im, s5_log_step, s5_b_re, s5_b_im, s5_c_re, s5_c_im, s5_d, s5_glu_w,
               s5_glu_b, w_branch, w_out, final_g, h0=h0)
    return y[:bp * lp].reshape(bp, lp, d), y[bp * lp:].reshape(bs, ls, d)
```
